```python
import jax
import jax.numpy as jnp
from jax import lax


D_MODEL = 1024
BATCH = 16
SEQ = 2048
DEPTH = 1

PLE_DIM = 256
LN_EPS = 1e-5
DEEPNORM_ALPHA = (2 * DEPTH) ** 0.25
DEEPNORM_BETA = (8 * DEPTH) ** -0.25

GM_CHUNK = 128
GM_GROUPS = 8
GM_WIDTH = D_MODEL
GM_GROUP_DIM = GM_WIDTH // GM_GROUPS

SSD_INNER = 2 * D_MODEL
SSD_HEAD_DIM = 64
SSD_HEADS = SSD_INNER // SSD_HEAD_DIM
SSD_GROUPS = 8
SSD_HEADS_PER_GROUP = SSD_HEADS // SSD_GROUPS
SSD_STATE = 128
SSD_CONV = 4
SSD_CHUNK = 128
SSD_CONV_DIM = SSD_INNER + 2 * SSD_GROUPS * SSD_STATE

PEER_HEADS = 8
PEER_NKEYS = 128
PEER_EXPERTS = PEER_NKEYS * PEER_NKEYS
PEER_DKEY = 256
PEER_HALF = PEER_DKEY // 2
PEER_TOPK = 16
PEER_TOKEN_BLOCK = 128

N_BRANCHES = 2
IN_COLS = 2 * GM_WIDTH + SSD_INNER + SSD_CONV_DIM + SSD_HEADS + N_BRANCHES * D_MODEL

kernel_name = 'hybrid_gmlp_ssd_peer_block'


def layer_norm(x, g, b):
    xf = x.astype(jnp.float32)
    mu = jnp.mean(xf, axis=-1, keepdims=True)
    var = jnp.mean(jnp.square(xf - mu), axis=-1, keepdims=True)
    return ((xf - mu) * lax.rsqrt(var + LN_EPS) * g + b).astype(x.dtype)


def gmlp_branch(uv, ln_g, ln_b, w_s, b_s):
    u, v = jnp.split(jax.nn.gelu(uv), 2, axis=-1)
    v = layer_norm(v, ln_g, ln_b)
    bsz, s, _ = v.shape
    nc = s // GM_CHUNK
    v = v.reshape(bsz, nc, GM_CHUNK, GM_GROUPS, GM_GROUP_DIM)
    causal = jnp.tril(jnp.ones((GM_CHUNK, GM_CHUNK), dtype=bool))
    w = jnp.where(causal[None], w_s, jnp.zeros_like(w_s))
    mixed = jnp.einsum('gts,bcsgd->bctgd', w, v) + b_s.T[None, None, :, :, None]
    return u * mixed.reshape(bsz, s, GM_WIDTH).astype(u.dtype)


def causal_depthwise_conv(x, w, b):
    c = x.shape[-1]
    out = lax.conv_general_dilated(
        x, w[:, None, :].astype(x.dtype), window_strides=(1,), padding=[(SSD_CONV - 1, 0)],
        dimension_numbers=('NWC', 'WIO', 'NWC'), feature_group_count=c)
    return out + b


def ssd_branch(z, xbc, dt_raw, conv_w, conv_b, dt_bias, a_log, d_skip, norm_w):
    xbc = jax.nn.silu(causal_depthwise_conv(xbc, conv_w, conv_b))
    xs, bm, cm = jnp.split(xbc, [SSD_INNER, SSD_INNER + SSD_GROUPS * SSD_STATE], axis=-1)
    bsz, s, _ = xs.shape
    nc = s // SSD_CHUNK
    L = SSD_CHUNK
    G, HG, P, N = SSD_GROUPS, SSD_HEADS_PER_GROUP, SSD_HEAD_DIM, SSD_STATE
    x = xs.reshape(bsz, nc, L, G, HG, P)
    bm = bm.reshape(bsz, nc, L, G, N)
    cm = cm.reshape(bsz, nc, L, G, N)
    dt = jax.nn.softplus(dt_raw.astype(jnp.float32) + dt_bias).reshape(bsz, nc, L, G, HG)
    a = -jnp.exp(a_log.astype(jnp.float32)).reshape(G, HG)
    a_cum = jnp.cumsum(dt * a, axis=2)
    a_cum_t = jnp.transpose(a_cum, (0, 1, 3, 4, 2))
    xdt = x * dt[..., None]
    causal = jnp.tril(jnp.ones((L, L), dtype=bool))
    seg = a_cum_t[..., :, None] - a_cum_t[..., None, :]
    decay = jnp.exp(jnp.where(causal, seg, -jnp.inf))
    cb = jnp.einsum('bclgn,bcsgn->bcgls', cm, bm)
    y_diag = jnp.einsum('bcgls,bcghls,bcsghp->bclghp', cb, decay, xdt)
    decay_states = jnp.exp(a_cum_t[..., -1:] - a_cum_t)
    states = jnp.einsum('bclgn,bcghl,bclghp->bcghpn', bm, decay_states, xdt)
    chunk_decay = jnp.exp(a_cum_t[..., -1])

    def step(h, inp):
        dec, st = inp
        return dec[..., None, None] * h + st, h

    h0 = jnp.zeros((bsz, G, HG, P, N), states.dtype)
    _, prev = lax.scan(step, h0, (jnp.moveaxis(chunk_decay, 1, 0), jnp.moveaxis(states, 1, 0)))
    prev = jnp.moveaxis(prev, 0, 1)
    y_off = jnp.einsum('bclgn,bcghpn,bcghl->bclghp', cm, prev, jnp.exp(a_cum_t))
    y = y_diag + y_off + x * d_skip.reshape(G, HG)[..., None]
    y = y.reshape(bsz, s, SSD_INNER)
    yf = (y * jax.nn.silu(z)).astype(jnp.float32)
    yf = yf * lax.rsqrt(jnp.mean(jnp.square(yf), axis=-1, keepdims=True) + LN_EPS) * norm_w
    return yf.astype(z.dtype)


def peer(x, w_q, sub_k1, sub_k2, u_tab, v_tab):
    bsz, s, d = x.shape
    T = bsz * s
    xt = x.reshape(T, d)
    q = (xt @ w_q).reshape(T, PEER_HEADS, 2, PEER_HALF)
    s1 = jnp.einsum('thd,hkd->thk', q[:, :, 0], sub_k1).astype(jnp.float32)
    s2 = jnp.einsum('thd,hkd->thk', q[:, :, 1], sub_k2).astype(jnp.float32)
    v1, i1 = lax.top_k(s1, PEER_TOPK)
    v2, i2 = lax.top_k(s2, PEER_TOPK)
    cand = (v1[..., :, None] + v2[..., None, :]).reshape(T, PEER_HEADS, PEER_TOPK * PEER_TOPK)
    cand_idx = (i1[..., :, None] * PEER_NKEYS + i2[..., None, :]).reshape(T, PEER_HEADS, PEER_TOPK * PEER_TOPK)
    best, pos = lax.top_k(cand, PEER_TOPK)
    experts = jnp.take_along_axis(cand_idx, pos, axis=-1)
    gates = jax.nn.softmax(best, axis=-1)
    nb = T // PEER_TOKEN_BLOCK
    hk = PEER_HEADS * PEER_TOPK

    def block(args):
        xb, eb, gb = args
        u = u_tab[eb]
        act = jax.nn.gelu(jnp.einsum('tkd,td->tk', u, xb).astype(jnp.float32))
        return jnp.einsum('tk,tkd->td', (gb * act).astype(x.dtype), v_tab[eb])

    out = lax.map(block, (xt.reshape(nb, PEER_TOKEN_BLOCK, d),
                          experts.reshape(nb, PEER_TOKEN_BLOCK, hk),
                          gates.reshape(nb, PEER_TOKEN_BLOCK, hk)))
    return out.reshape(bsz, s, d).astype(x.dtype)


def setup_inputs(seed: int = 0) -> dict:
    key = jax.random.key(seed)
    ks = jax.random.split(key, 32)
    L = DEPTH
    f32 = jnp.float32

    def nrm(k, shape, scale):
        return jax.random.normal(k, shape, f32) * scale

    dt0 = jnp.exp(jax.random.uniform(ks[10], (L, SSD_HEADS), f32, jnp.log(1e-3), jnp.log(1e-1)))
    return {
        'x': nrm(ks[0], (BATCH, SEQ, D_MODEL), 1.0),
        'p': nrm(ks[1], (DEPTH, BATCH, SEQ, PLE_DIM), 1.0),
        'w_in': nrm(ks[2], (L, D_MODEL, IN_COLS), D_MODEL ** -0.5),
        'b_gate': nrm(ks[3], (L, N_BRANCHES * D_MODEL), 0.1),
        'gm_ln_g': 1.0 + nrm(ks[4], (L, GM_WIDTH), 0.02),
        'gm_ln_b': nrm(ks[5], (L, GM_WIDTH), 0.02),
        'gm_w_s': nrm(ks[6], (L, GM_GROUPS, GM_CHUNK, GM_CHUNK), GM_CHUNK ** -0.5),
        'gm_b_s': 1.0 + nrm(ks[7], (L, GM_GROUPS, GM_CHUNK), 0.02),
        'gm_w_out': nrm(ks[8], (L, GM_WIDTH, D_MODEL), GM_WIDTH ** -0.5),
        'ssd_conv_w': nrm(ks[9], (L, SSD_CONV, SSD_CONV_DIM), SSD_CONV ** -0.5),
        'ssd_conv_b': nrm(ks[11], (L, SSD_CONV_DIM), 0.02),
        'ssd_dt_bias': dt0 + jnp.log(-jnp.expm1(-dt0)),
        'ssd_a_log': jnp.log(jax.random.uniform(ks[12], (L, SSD_HEADS), f32, 1.0, 16.0)),
        'ssd_d': 1.0 + nrm(ks[13], (L, SSD_HEADS), 0.02),
        'ssd_norm_w': 1.0 + nrm(ks[14], (L, SSD_INNER), 0.02),
        'ssd_w_out': nrm(ks[15], (L, SSD_INNER, D_MODEL), SSD_INNER ** -0.5),
        'w_o': nrm(ks[16], (L, D_MODEL, D_MODEL), D_MODEL ** -0.5 * DEEPNORM_BETA),
        'ln1_g': 1.0 + nrm(ks[17], (L, D_MODEL), 0.02),
        'ln1_b': nrm(ks[18], (L, D_MODEL), 0.02),
        'peer_w_q': nrm(ks[19], (L, D_MODEL, PEER_HEADS * PEER_DKEY), D_MODEL ** -0.5),
        'peer_k1': nrm(ks[20], (L, PEER_HEADS, PEER_NKEYS, PEER_HALF), PEER_HALF ** -0.5),
        'peer_k2': nrm(ks[21], (L, PEER_HEADS, PEER_NKEYS, PEER_HALF), PEER_HALF ** -0.5),
        'peer_u': nrm(ks[22], (L, PEER_EXPERTS, D_MODEL), D_MODEL ** -0.5),
        'peer_v': nrm(ks[23], (L, PEER_EXPERTS, D_MODEL), DEEPNORM_BETA * PEER_HEADS ** -0.5),
        'ple_w_proj': nrm(ks[24], (L, PLE_DIM, D_MODEL), PLE_DIM ** -0.5 * DEEPNORM_BETA),
        'ple_w_gate': nrm(ks[25], (L, D_MODEL, D_MODEL), D_MODEL ** -0.5),
        'ple_b_gate': nrm(ks[26], (L, D_MODEL), 0.1),
        'ln2_g': 1.0 + nrm(ks[27], (L, D_MODEL), 0.02),
        'ln2_b': nrm(ks[28], (L, D_MODEL), 0.02),
    }


def reference(x, p, w_in, b_gate, gm_ln_g, gm_ln_b, gm_w_s, gm_b_s, gm_w_out,
              ssd_conv_w, ssd_conv_b, ssd_dt_bias, ssd_a_log, ssd_d, ssd_norm_w, ssd_w_out,
              w_o, ln1_g, ln1_b, peer_w_q, peer_k1, peer_k2, peer_u, peer_v,
              ple_w_proj, ple_w_gate, ple_b_gate, ln2_g, ln2_b):
    bsz, s, d = x.shape
    o1 = 2 * GM_WIDTH
    o2 = o1 + SSD_INNER
    o3 = o2 + SSD_CONV_DIM
    o4 = o3 + SSD_HEADS
    for i in range(DEPTH):
        h = x @ w_in[i]
        uv, z, xbc, dt_raw, gate_pre = jnp.split(h, [o1, o2, o3, o4], axis=-1)
        gates = jax.nn.sigmoid((gate_pre + b_gate[i]).astype(jnp.float32)).reshape(bsz, s, N_BRANCHES, d)
        ya = gmlp_branch(uv, gm_ln_g[i], gm_ln_b[i], gm_w_s[i], gm_b_s[i]) @ gm_w_out[i]
        yb = ssd_branch(z, xbc, dt_raw, ssd_conv_w[i], ssd_conv_b[i], ssd_dt_bias[i],
                        ssd_a_log[i], ssd_d[i], ssd_norm_w[i]) @ ssd_w_out[i]
        merged = (gates[..., 0, :] * ya + gates[..., 1, :] * yb).astype(x.dtype)
        x1 = layer_norm(DEEPNORM_ALPHA * x + merged @ w_o[i], ln1_g[i], ln1_b[i])
        ch = peer(x1, peer_w_q[i], peer_k1[i], peer_k2[i], peer_u[i], peer_v[i])
        ple_gate = jax.nn.sigmoid((x1 @ ple_w_gate[i] + ple_b_gate[i]).astype(jnp.float32))
        ple = (ple_gate * (p[i] @ ple_w_proj[i])).astype(x.dtype)
        x = layer_norm(DEEPNORM_ALPHA * x1 + ch + ple, ln2_g[i], ln2_b[i])
    return x
```

```python
import functools
import math

import jax
import jax.numpy as jnp
from jax import lax
from jax.experimental import pallas as pl
from jax.experimental.pallas import tpu as pltpu

BF16 = jnp.bfloat16
F32 = jnp.float32

LN_EPS = 1e-5
CHUNK = 128
GM_GROUPS = 8
SSD_HEAD_DIM = 64
SSD_STATE = 128
SSD_GROUPS = 8
SSD_CONV = 4
PEER_HEADS = 8
PEER_NKEYS = 128
PEER_TOPK = 16
LANES = 128
SUBLANES = 8
VMEM_LIMIT = 60 * 1024 * 1024

NT_DIMS = (((1,), (1,)), ((), ()))


def _cparams(sem):
    return pltpu.CompilerParams(dimension_semantics=sem, vmem_limit_bytes=VMEM_LIMIT)


def _layer_norm(t, g, b):
    mu = jnp.mean(t, axis=-1, keepdims=True)
    d = t - mu
    var = jnp.mean(d * d, axis=-1, keepdims=True)
    return d * lax.rsqrt(var + LN_EPS) * g + b


def _split3(v):
    hi = v.astype(BF16)
    r = v - hi.astype(F32)
    mid = r.astype(BF16)
    lo = (r - mid.astype(F32)).astype(BF16)
    return hi, mid, lo


def _inproj_kernel(x_ref, w_ref, o_ref):
    o_ref[...] = jnp.dot(x_ref[...].astype(BF16), w_ref[...], preferred_element_type=F32)


def _inproj(x2d, w):
    t, d = x2d.shape
    n = w.shape[1]
    tm = min(1024, t)
    tn = 512
    return pl.pallas_call(
        _inproj_kernel,
        out_shape=jax.ShapeDtypeStruct((t, n), F32),
        grid=(t // tm, n // tn),
        in_specs=[pl.BlockSpec((tm, d), lambda i, j: (i, 0)),
                  pl.BlockSpec((d, tn), lambda i, j: (0, j))],
        out_specs=pl.BlockSpec((tm, tn), lambda i, j: (i, j)),
        compiler_params=_cparams(("parallel", "arbitrary")),
        name="inproj",
    )(x2d, w)


def _gmlp_kernel(uv_ref, gp_ref, lng_ref, lnb_ref, wc_ref, bs_ref, wout_ref, bg_ref,
                 o_ref, pre_ref):
    width = lng_ref.shape[1]
    gd = width // GM_GROUPS
    for c in range(uv_ref.shape[0] // CHUNK):
        rows = slice(c * CHUNK, (c + 1) * CHUNK)
        g = jax.nn.gelu(uv_ref[rows, :])
        u = g[:, :width]
        v = _layer_norm(g[:, width:], lng_ref[...], lnb_ref[...]).astype(BF16)
        for grp in range(GM_GROUPS):
            cols = slice(grp * gd, (grp + 1) * gd)
            mixed = jnp.dot(wc_ref[grp], v[:, cols], preferred_element_type=F32)
            pre_ref[rows, cols] = (u[:, cols] * (mixed + bs_ref[:, cols])).astype(BF16)
    ya = jnp.dot(pre_ref[...], wout_ref[...], preferred_element_type=F32)
    o_ref[...] = jax.nn.sigmoid(gp_ref[...] + bg_ref[...]) * ya


def _gmlp(h, ln_g, ln_b, wc, bs_exp, w_out, b_gate_a, gate_col_block):
    t = h.shape[0]
    width = ln_g.shape[1]
    tm = min(512, t)
    full = lambda shape: pl.BlockSpec(shape, lambda i: (0,) * len(shape))
    return pl.pallas_call(
        _gmlp_kernel,
        out_shape=jax.ShapeDtypeStruct((t, width), F32),
        grid=(t // tm,),
        in_specs=[pl.BlockSpec((tm, 2 * width), lambda i: (i, 0)),
                  pl.BlockSpec((tm, width), lambda i: (i, gate_col_block)),
                  full(ln_g.shape), full(ln_b.shape), full(wc.shape), full(bs_exp.shape),
                  full(w_out.shape), full(b_gate_a.shape)],
        out_specs=pl.BlockSpec((tm, width), lambda i: (i, 0)),
        scratch_shapes=[pltpu.VMEM((tm, width), BF16)],
        compiler_params=_cparams(("parallel",)),
        name="gmlp",
    )(h, h, ln_g, ln_b, wc, bs_exp, w_out, b_gate_a)


def _ssd_kernel(x_ref, z_ref, xbc_ref, wdt_ref, wdtT_ref, dtb_row_ref, dtb_col_ref,
                alog_row_ref, alog_col_ref, convw_ref, convb_ref, dexp_ref, normw_ref,
                e64_ref, e128_ref, ltri_ref, utri_ref,
                o_ref, xe_ref, st_ref, y_ref):
    inner = z_ref.shape[1]
    n_pairs = inner // LANES
    gs = SSD_GROUPS * SSD_STATE

    @pl.when(pl.program_id(1) == 0)
    def _():
        xe_ref[0:SUBLANES, :] = jnp.zeros((SUBLANES, xe_ref.shape[1]), F32)
        st_ref[...] = jnp.zeros(st_ref.shape, F32)

    xb = x_ref[...].astype(BF16)
    dt = jax.nn.softplus(jnp.dot(xb, wdt_ref[...], preferred_element_type=F32) + dtb_row_ref[...])
    dt_t = jax.nn.softplus(lax.dot_general(wdtT_ref[...], xb, NT_DIMS, preferred_element_type=F32)
                           + dtb_col_ref[...])
    da = dt * (-jnp.exp(alog_row_ref[...]))
    da_t = dt_t * (-jnp.exp(alog_col_ref[...]))
    da_parts = _split3(da)
    ltri = ltri_ref[...]
    acum = sum(jnp.dot(ltri, p, preferred_element_type=F32) for p in da_parts)
    utri = utri_ref[...]
    acum_t = sum(jnp.dot(p, utri, preferred_element_type=F32) for p in _split3(da_t))

    e64 = e64_ref[...]
    e128 = e128_ref[...]
    acum_parts = _split3(acum)
    dt_e = sum(jnp.dot(p, e64, preferred_element_type=F32) for p in _split3(dt))
    acum_e = sum(jnp.dot(p, e64, preferred_element_type=F32) for p in acum_parts)
    acum_b = sum(jnp.dot(p, e128, preferred_element_type=F32) for p in acum_parts)
    alast_e = acum_e[CHUNK - 1:CHUNK, :]

    xe_ref[SUBLANES:SUBLANES + CHUNK, :] = xbc_ref[...]
    conv = convb_ref[...]
    for k in range(SSD_CONV):
        off = SUBLANES - (SSD_CONV - 1) + k
        conv = conv + convw_ref[k:k + 1, :] * xe_ref[off:off + CHUNK, :]
    xe_ref[0:SUBLANES, :] = xe_ref[CHUNK:CHUNK + SUBLANES, :]
    xbc = jax.nn.silu(conv)

    xs = xbc[:, :inner]
    xdt = xs * dt_e
    xdt_b = xdt.astype(BF16)
    xw_b = (xdt * jnp.exp(alast_e - acum_e)).astype(BF16)
    ea = jnp.exp(acum_e)
    dec_state = jnp.exp(alast_e)
    dskip = dexp_ref[...]

    lane = lax.broadcasted_iota(jnp.int32, (CHUNK, LANES), 1)
    row = lax.broadcasted_iota(jnp.int32, (CHUNK, LANES), 0)
    causal = row >= lane
    first_half = lane < SSD_HEAD_DIM
    pairs_per_group = n_pairs // SSD_GROUPS

    for g in range(SSD_GROUPS):
        bm_f = xbc[:, inner + g * SSD_STATE: inner + (g + 1) * SSD_STATE]
        bm_g = bm_f.astype(BF16)
        bmt_g = bm_f.T.astype(BF16)
        cm_g = xbc[:, inner + gs + g * SSD_STATE: inner + gs + (g + 1) * SSD_STATE].astype(BF16)
        cb = lax.dot_general(cm_g, bm_g, NT_DIMS, preferred_element_type=F32)
        for j in range(pairs_per_group):
            pr = g * pairs_per_group + j
            cols = slice(pr * LANES, (pr + 1) * LANES)
            xdt_p = xdt_b[:, cols]
            y_p = None
            for k in range(2):
                hd = pr * 2 + k
                seg = acum_b[:, hd * LANES:(hd + 1) * LANES] - acum_t[hd:hd + 1, :]
                dec = jnp.exp(jnp.where(causal, seg, -jnp.inf))
                sc = (cb * dec).astype(BF16)
                keep = first_half if k == 0 else jnp.logical_not(first_half)
                rhs = jnp.where(keep, xdt_p, jnp.zeros_like(xdt_p))
                yk = jnp.dot(sc, rhs, preferred_element_type=F32)
                y_p = yk if y_p is None else y_p + yk
            st = st_ref[pr]
            y_off = jnp.dot(cm_g, st.astype(BF16), preferred_element_type=F32) * ea[:, cols]
            y_ref[:, cols] = y_p + y_off + xs[:, cols] * dskip[:, cols]
            upd = jnp.dot(bmt_g, xw_b[:, cols], preferred_element_type=F32)
            st_ref[pr] = dec_state[:, cols] * st + upd

    yf = y_ref[...] * jax.nn.silu(z_ref[...])
    ms = jnp.mean(yf * yf, axis=-1, keepdims=True)
    o_ref[...] = yf * lax.rsqrt(ms + LN_EPS) * normw_ref[...]


def _ssd(x2d, h, bsz, consts, z_col_block, xbc_col_block, inner, conv_dim):
    t, d = x2d.shape
    nc = t // bsz // CHUNK
    n_pairs = inner // LANES
    row_map = lambda b, c: (b * nc + c, 0)
    full = lambda a: pl.BlockSpec(a.shape, lambda b, c: (0,) * a.ndim)
    return pl.pallas_call(
        _ssd_kernel,
        out_shape=jax.ShapeDtypeStruct((t, inner), F32),
        grid=(bsz, nc),
        in_specs=[pl.BlockSpec((CHUNK, d), row_map),
                  pl.BlockSpec((CHUNK, inner), lambda b, c: (b * nc + c, z_col_block)),
                  pl.BlockSpec((CHUNK, conv_dim), lambda b, c: (b * nc + c, xbc_col_block))]
                 + [full(a) for a in consts],
        out_specs=pl.BlockSpec((CHUNK, inner), row_map),
        scratch_shapes=[pltpu.VMEM((CHUNK + SUBLANES, conv_dim), F32),
                        pltpu.VMEM((n_pairs, SSD_STATE, LANES), F32),
                        pltpu.VMEM((CHUNK, inner), F32)],
        compiler_params=_cparams(("parallel", "arbitrary")),
        name="ssd",
    )(x2d, h, h, *consts)


def _merge_kernel(alpha, ya_ref, yn_ref, gp_ref, bg_ref, wssd_ref, wo_ref, x_ref, lng_ref, lnb_ref,
                  o_ref):
    yb = jnp.dot(yn_ref[...].astype(BF16), wssd_ref[...], preferred_element_type=F32)
    merged = ya_ref[...] + jax.nn.sigmoid(gp_ref[...] + bg_ref[...]) * yb
    t = alpha * x_ref[...] + jnp.dot(merged.astype(BF16), wo_ref[...], preferred_element_type=F32)
    o_ref[...] = _layer_norm(t, lng_ref[...], lnb_ref[...])


def _merge(alpha, ya_g, y_norm, h, b_gate_b, w_ssd, w_o, x2d, ln_g, ln_b, gate_col_block):
    t, d = x2d.shape
    inner = y_norm.shape[1]
    tm = min(256, t)
    full = lambda a: pl.BlockSpec(a.shape, lambda i: (0,) * a.ndim)
    return pl.pallas_call(
        functools.partial(_merge_kernel, alpha),
        out_shape=jax.ShapeDtypeStruct((t, d), F32),
        grid=(t // tm,),
        in_specs=[pl.BlockSpec((tm, d), lambda i: (i, 0)),
                  pl.BlockSpec((tm, inner), lambda i: (i, 0)),
                  pl.BlockSpec((tm, d), lambda i: (i, gate_col_block)),
                  full(b_gate_b), full(w_ssd), full(w_o),
                  pl.BlockSpec((tm, d), lambda i: (i, 0)),
                  full(ln_g), full(ln_b)],
        out_specs=pl.BlockSpec((tm, d), lambda i: (i, 0)),
        compiler_params=_cparams(("parallel",)),
        name="merge",
    )(ya_g, y_norm, h, b_gate_b, w_ssd, w_o, x2d, ln_g, ln_b)


def _bitonic_merge_desc(xs):
    xs = list(xs)
    n = len(xs)
    j = n // 2
    while j >= 1:
        for i in range(n):
            l = i ^ j
            if l > i:
                xs[i], xs[l] = jnp.maximum(xs[i], xs[l]), jnp.minimum(xs[i], xs[l])
        j //= 2
    return xs


def _bitonic_sort_desc(xs):
    xs = list(xs)
    n = len(xs)
    k = 2
    while k <= n:
        j = k // 2
        while j >= 1:
            for i in range(n):
                l = i ^ j
                if l > i:
                    hi, lo = jnp.maximum(xs[i], xs[l]), jnp.minimum(xs[i], xs[l])
                    xs[i], xs[l] = (hi, lo) if (i & k) == 0 else (lo, hi)
            j //= 2
        k *= 2
    return xs


def _top16_and_next(s):
    nblk = s.shape[0] // SUBLANES
    blocks = [s[i * SUBLANES:(i + 1) * SUBLANES, :] for i in range(nblk)]
    top = _bitonic_sort_desc(blocks)
    for shift in (4, 2, 1):
        other = [pltpu.roll(r, shift, axis=0) for r in top]
        top = _bitonic_merge_desc([jnp.maximum(top[i], other[nblk - 1 - i]) for i in range(nblk)])
    kth = top[PEER_TOPK - 1]
    nxt = None
    for blk in blocks:
        cand = jnp.where(blk < kth, blk, -jnp.inf)
        nxt = cand if nxt is None else jnp.maximum(nxt, cand)
    for shift in (4, 2, 1):
        nxt = jnp.maximum(nxt, pltpu.roll(nxt, shift, axis=0))
    return top + [nxt]


def _peer_prep_kernel(alpha, x1_ref, p_ref, wqT_ref, k1_ref, k2_ref, wg_ref, bg_ref, wp_ref,
                      base_ref, xb_ref, thr_ref, cw_ref, s2o_ref, e2_ref,
                      qT_ref, s1_ref, top1_ref, top2_ref, row_ref):
    tm = x1_ref.shape[0]
    x1 = x1_ref[...]
    xb = x1.astype(BF16)
    xb_ref[...] = xb
    gate = jax.nn.sigmoid(jnp.dot(xb, wg_ref[...], preferred_element_type=F32) + bg_ref[...])
    ple = gate * jnp.dot(p_ref[...].astype(BF16), wp_ref[...], preferred_element_type=F32)
    base_ref[...] = alpha * x1 + ple

    qT_ref[...] = lax.dot_general(wqT_ref[...], xb, NT_DIMS, preferred_element_type=F32).astype(BF16)
    top1_ref[...] = jnp.zeros(top1_ref.shape, F32)
    top2_ref[...] = jnp.zeros(top2_ref.shape, F32)
    sub = lax.broadcasted_iota(jnp.int32, (SUBLANES, tm), 0)
    dkey = 2 * PEER_NKEYS

    def head_scores(hd, carry):
        q1 = qT_ref[pl.ds(pl.multiple_of(hd * dkey, dkey), PEER_NKEYS), :]
        q2 = qT_ref[pl.ds(pl.multiple_of(hd * dkey + PEER_NKEYS, PEER_NKEYS), PEER_NKEYS), :]
        s1 = jnp.dot(k1_ref[hd], q1, preferred_element_type=F32)
        s2 = jnp.dot(k2_ref[hd], q2, preferred_element_type=F32)
        s1_ref[hd] = s1
        s2o_ref[hd] = s2
        mine = sub == hd
        for i, v in enumerate(_top16_and_next(s1)):
            top1_ref[i] = jnp.where(mine, v, top1_ref[i])
        for i, v in enumerate(_top16_and_next(s2)):
            top2_ref[i] = jnp.where(mine, v, top2_ref[i])
        return carry

    lax.fori_loop(0, PEER_HEADS, head_scores, 0)

    nt = PEER_TOPK + 1
    t1 = [top1_ref[i] for i in range(nt)]
    t2 = [top2_ref[i] for i in range(nt)]
    cands = [t1[a] + t2[b] for a in range(nt) for b in range(nt) if (a + 1) * (b + 1) <= nt]
    cands += [jnp.full((SUBLANES, tm), -jnp.inf, F32)] * (64 - len(cands))
    srt = _bitonic_sort_desc(cands)
    zsum = None
    for i in range(PEER_TOPK):
        e = jnp.exp(srt[i] - srt[0])
        zsum = e if zsum is None else zsum + e
    row_ref[0] = 0.5 * (srt[PEER_TOPK - 1] + srt[PEER_TOPK])
    row_ref[1] = t1[0]
    row_ref[2] = t2[0]
    row_ref[3] = 1.0 / zsum

    def head_factors(hd, carry):
        theta = row_ref[0, pl.ds(hd, 1), :]
        m1 = row_ref[1, pl.ds(hd, 1), :]
        m2 = row_ref[2, pl.ds(hd, 1), :]
        inv_z = row_ref[3, pl.ds(hd, 1), :]
        s1 = s1_ref[hd]
        thr_ref[hd] = theta - s1
        cw_ref[hd] = jnp.exp(s1 - m1) * inv_z
        e2_ref[hd] = jnp.exp(s2o_ref[hd] - m2)
        return carry

    lax.fori_loop(0, PEER_HEADS, head_factors, 0)


def _peer_prep(alpha, x1, p2d, wqT, k1, k2, wg, bg, wp):
    t, d = x1.shape
    tm = min(256, t)
    full = lambda a: pl.BlockSpec(a.shape, lambda i: (0,) * a.ndim)
    head_shape = jax.ShapeDtypeStruct((PEER_HEADS, PEER_NKEYS, t), F32)
    head_spec = pl.BlockSpec((PEER_HEADS, PEER_NKEYS, tm), lambda i: (0, 0, i))
    nt = PEER_TOPK + 1
    return pl.pallas_call(
        functools.partial(_peer_prep_kernel, alpha),
        out_shape=(jax.ShapeDtypeStruct((t, d), F32), jax.ShapeDtypeStruct((t, d), BF16),
                   head_shape, head_shape, head_shape, head_shape),
        grid=(t // tm,),
        in_specs=[pl.BlockSpec((tm, d), lambda i: (i, 0)),
                  pl.BlockSpec((tm, p2d.shape[1]), lambda i: (i, 0)),
                  full(wqT), full(k1), full(k2), full(wg), full(bg), full(wp)],
        out_specs=(pl.BlockSpec((tm, d), lambda i: (i, 0)), pl.BlockSpec((tm, d), lambda i: (i, 0)),
                   head_spec, head_spec, head_spec, head_spec),
        scratch_shapes=[pltpu.VMEM((wqT.shape[0], tm), BF16),
                        pltpu.VMEM((PEER_HEADS, PEER_NKEYS, tm), F32),
                        pltpu.VMEM((nt, SUBLANES, tm), F32),
                        pltpu.VMEM((nt, SUBLANES, tm), F32),
                        pltpu.VMEM((4, SUBLANES, tm), F32)],
        compiler_params=_cparams(("parallel",)),
        name="peer_prep",
    )(x1, p2d, wqT, k1, k2, wg, bg, wp)


def _peer_dense_kernel(xb_ref, base_ref, thr_ref, cw_ref, s2_ref, e2_ref, u_ref, vt_ref,
                       lng_ref, lnb_ref, o_ref, acc_ref, m_ref):
    j = pl.program_id(1)
    ec = u_ref.shape[0]
    sub_chunks = ec // PEER_NKEYS

    @pl.when(j == 0)
    def _():
        acc_ref[...] = jnp.zeros(acc_ref.shape, F32)

    ht = lax.dot_general(u_ref[...], xb_ref[...], NT_DIMS, preferred_element_type=F32)
    act = jax.nn.gelu(ht)
    for i in range(sub_chunks):
        e1 = j * sub_chunks + i
        w = None
        for hd in range(PEER_HEADS):
            thr = thr_ref[hd, pl.ds(e1, 1), :]
            cw = cw_ref[hd, pl.ds(e1, 1), :]
            term = jnp.where(s2_ref[hd] >= thr, e2_ref[hd] * cw, 0.0)
            w = term if w is None else w + term
        rows = slice(i * PEER_NKEYS, (i + 1) * PEER_NKEYS)
        m_ref[rows, :] = (act[rows, :] * w).astype(BF16)
    acc_ref[...] += jnp.dot(vt_ref[...], m_ref[...], preferred_element_type=F32)

    @pl.when(j == pl.num_programs(1) - 1)
    def _():
        y = base_ref[...] + acc_ref[...].T
        o_ref[...] = _layer_norm(y, lng_ref[...], lnb_ref[...])


def _peer_dense(xb, base, thr, cw, s2, e2, u_bf, vt_bf, ln_g, ln_b):
    t, d = base.shape
    ne = u_bf.shape[0]
    tb = min(512, t)
    ec = 512
    full = lambda a: pl.BlockSpec(a.shape, lambda i, j: (0,) * a.ndim)
    head_spec = pl.BlockSpec((PEER_HEADS, PEER_NKEYS, tb), lambda i, j: (0, 0, i))
    return pl.pallas_call(
        _peer_dense_kernel,
        out_shape=jax.ShapeDtypeStruct((t, d), F32),
        grid=(t // tb, ne // ec),
        in_specs=[pl.BlockSpec((tb, d), lambda i, j: (i, 0)),
                  pl.BlockSpec((tb, d), lambda i, j: (i, 0)),
                  head_spec, head_spec, head_spec, head_spec,
                  pl.BlockSpec((ec, d), lambda i, j: (j, 0)),
                  pl.BlockSpec((d, ec), lambda i, j: (0, j)),
                  full(ln_g), full(ln_b)],
        out_specs=pl.BlockSpec((tb, d), lambda i, j: (i, 0)),
        scratch_shapes=[pltpu.VMEM((d, tb), F32), pltpu.VMEM((ec, tb), BF16)],
        compiler_params=_cparams(("parallel", "arbitrary")),
        name="peer_dense",
    )(xb, base, thr, cw, s2, e2, u_bf, vt_bf, ln_g, ln_b)


def _pad_cols(a, n):
    return jnp.pad(a, ((0, 0), (0, n - a.shape[1])))


def _layer(x2d, p2d, bsz, alpha, w_in, b_gate, gm_ln_g, gm_ln_b, gm_w_s, gm_b_s, gm_w_out,
           ssd_conv_w, ssd_conv_b, ssd_dt_bias, ssd_a_log, ssd_d, ssd_norm_w, ssd_w_out,
           w_o, ln1_g, ln1_b, peer_w_q, peer_k1, peer_k2, peer_u, peer_v,
           ple_w_proj, ple_w_gate, ple_b_gate, ln2_g, ln2_b):
    d = x2d.shape[1]
    width = gm_ln_g.shape[0]
    inner = ssd_norm_w.shape[0]
    conv_dim = ssd_conv_b.shape[0]
    heads = ssd_a_log.shape[0]
    o1 = 2 * width
    o2 = o1 + inner
    o3 = o2 + conv_dim
    o4 = o3 + heads
    row = lambda v: v.reshape(1, -1)

    w_main = jnp.concatenate([w_in[:, :o3], w_in[:, o4:]], axis=1).astype(BF16)
    h = _inproj(x2d, w_main)
    gate_a_block = o3 // d
    gate_b_block = gate_a_block + 1

    causal = jnp.tril(jnp.ones((CHUNK, CHUNK), dtype=bool))
    wc = jnp.where(causal[None], gm_w_s, 0.0).astype(BF16)
    bs_exp = jnp.repeat(gm_b_s.T, width // GM_GROUPS, axis=1)
    ya_g = _gmlp(h, row(gm_ln_g), row(gm_ln_b), wc, bs_exp, gm_w_out.astype(BF16),
                 row(b_gate[:d]), gate_a_block)

    w_dt = _pad_cols(w_in[:, o3:o4], LANES)
    pad_row = lambda v: _pad_cols(row(v), LANES)
    hid = jnp.arange(LANES)[:, None]
    e64 = (hid == (jnp.arange(inner)[None, :] // SSD_HEAD_DIM)).astype(BF16)
    e128 = (hid == (jnp.arange(heads * LANES)[None, :] // LANES)).astype(BF16)
    ltri = causal.astype(BF16)
    consts = (w_dt.astype(BF16), w_dt.T.astype(BF16), pad_row(ssd_dt_bias), pad_row(ssd_dt_bias).T,
              pad_row(ssd_a_log), pad_row(ssd_a_log).T, ssd_conv_w, row(ssd_conv_b),
              row(jnp.repeat(ssd_d, SSD_HEAD_DIM)), row(ssd_norm_w), e64, e128, ltri, ltri.T)
    y_norm = _ssd(x2d, h, bsz, consts, o1 // inner, o2 // conv_dim, inner, conv_dim)

    x1 = _merge(alpha, ya_g, y_norm, h, row(b_gate[d:]), ssd_w_out.astype(BF16), w_o.astype(BF16),
                x2d, row(ln1_g), row(ln1_b), gate_b_block)

    base, xb, thr, cw, s2, e2 = _peer_prep(
        alpha, x1, p2d, peer_w_q.T.astype(BF16), peer_k1.astype(BF16), peer_k2.astype(BF16),
        ple_w_gate.astype(BF16), row(ple_b_gate), ple_w_proj.astype(BF16))
    return _peer_dense(xb, base, thr, cw, s2, e2, peer_u.astype(BF16), peer_v.T.astype(BF16),
                       row(ln2_g), row(ln2_b))


def kernel(x, p, w_in, b_gate, gm_ln_g, gm_ln_b, gm_w_s, gm_b_s, gm_w_out, ssd_conv_w, ssd_conv_b,
           ssd_dt_bias, ssd_a_log, ssd_d, ssd_norm_w, ssd_w_out, w_o, ln1_g, ln1_b, peer_w_q,
           peer_k1, peer_k2, peer_u, peer_v, ple_w_proj, ple_w_gate, ple_b_gate, ln2_g, ln2_b):
    bsz, s, d = x.shape
    depth = w_in.shape[0]
    alpha = (2 * depth) ** 0.25
    x2d = x.reshape(bsz * s, d)
    for i in range(depth):
        x2d = _layer(x2d, p[i].reshape(bsz * s, -1), bsz, alpha, w_in[i], b_gate[i], gm_ln_g[i],
                     gm_ln_b[i], gm_w_s[i], gm_b_s[i], gm_w_out[i], ssd_conv_w[i], ssd_conv_b[i],
                     ssd_dt_bias[i], ssd_a_log[i], ssd_d[i], ssd_norm_w[i], ssd_w_out[i], w_o[i],
                     ln1_g[i], ln1_b[i], peer_w_q[i], peer_k1[i], peer_k2[i], peer_u[i], peer_v[i],
                     ple_w_proj[i], ple_w_gate[i], ple_b_gate[i], ln2_g[i], ln2_b[i])
    return x2d.reshape(bsz, s, d)
```

```python
import functools
import math

import jax
import jax.numpy as jnp
from jax import lax
from jax.experimental import pallas as pl
from jax.experimental.pallas import tpu as pltpu

BF16 = jnp.bfloat16
F32 = jnp.float32

LN_EPS = 1e-5
CHUNK = 128
GM_GROUPS = 8
SSD_HEAD_DIM = 64
SSD_STATE = 128
SSD_GROUPS = 8
SSD_CONV = 4
PEER_HEADS = 8
PEER_NKEYS = 128
PEER_TOPK = 16
PACKED_KEYS = PEER_NKEYS // 2
LANES = 128
SUBLANES = 8
MXU_COLS = 256
VMEM_LIMIT = 60 * 1024 * 1024

NT_DIMS = (((1,), (1,)), ((), ()))


def _cparams(sem):
    return pltpu.CompilerParams(dimension_semantics=sem, vmem_limit_bytes=VMEM_LIMIT)


def _layer_norm(t, g, b):
    mu = jnp.mean(t, axis=-1, keepdims=True)
    d = t - mu
    var = jnp.mean(d * d, axis=-1, keepdims=True)
    return d * lax.rsqrt(var + LN_EPS) * g + b


def _split3(v):
    hi = v.astype(BF16)
    r = v - hi.astype(F32)
    mid = r.astype(BF16)
    lo = (r - mid.astype(F32)).astype(BF16)
    return hi, mid, lo


def _inproj_kernel(x_ref, w_ref, o_ref, xb_ref):
    @pl.when(pl.program_id(1) == 0)
    def _():
        xb_ref[...] = x_ref[...].astype(BF16)

    o_ref[...] = jnp.dot(xb_ref[...], w_ref[...], preferred_element_type=F32)


def _inproj(x2d, w):
    t, d = x2d.shape
    n = w.shape[1]
    tm = min(1024, t)
    tn = 512
    return pl.pallas_call(
        _inproj_kernel,
        out_shape=jax.ShapeDtypeStruct((t, n), F32),
        grid=(t // tm, n // tn),
        in_specs=[pl.BlockSpec((tm, d), lambda i, j: (i, 0)),
                  pl.BlockSpec((d, tn), lambda i, j: (0, j))],
        out_specs=pl.BlockSpec((tm, tn), lambda i, j: (i, j)),
        scratch_shapes=[pltpu.VMEM((tm, d), BF16)],
        compiler_params=_cparams(("parallel", "arbitrary")),
        name="inproj",
    )(x2d, w)


def _gmlp_kernel(uv_ref, gp_ref, lng_ref, lnb_ref, wc_ref, bs_ref, wout_ref, bg_ref,
                 o_ref, pre_ref):
    width = lng_ref.shape[1]
    gd = width // GM_GROUPS
    for c in range(uv_ref.shape[0] // CHUNK):
        rows = slice(c * CHUNK, (c + 1) * CHUNK)
        g = jax.nn.gelu(uv_ref[rows, :])
        u = g[:, :width]
        v = _layer_norm(g[:, width:], lng_ref[...], lnb_ref[...]).astype(BF16)
        for grp in range(GM_GROUPS):
            cols = slice(grp * gd, (grp + 1) * gd)
            mixed = jnp.dot(wc_ref[grp], v[:, cols], preferred_element_type=F32)
            pre_ref[rows, cols] = (u[:, cols] * (mixed + bs_ref[:, cols])).astype(BF16)
    ya = jnp.dot(pre_ref[...], wout_ref[...], preferred_element_type=F32)
    o_ref[...] = jax.nn.sigmoid(gp_ref[...] + bg_ref[...]) * ya


def _gmlp(h, ln_g, ln_b, wc, bs_exp, w_out, b_gate_a, gate_col_block):
    t = h.shape[0]
    width = ln_g.shape[1]
    tm = min(512, t)
    full = lambda shape: pl.BlockSpec(shape, lambda i: (0,) * len(shape))
    return pl.pallas_call(
        _gmlp_kernel,
        out_shape=jax.ShapeDtypeStruct((t, width), F32),
        grid=(t // tm,),
        in_specs=[pl.BlockSpec((tm, 2 * width), lambda i: (i, 0)),
                  pl.BlockSpec((tm, width), lambda i: (i, gate_col_block)),
                  full(ln_g.shape), full(ln_b.shape), full(wc.shape), full(bs_exp.shape),
                  full(w_out.shape), full(b_gate_a.shape)],
        out_specs=pl.BlockSpec((tm, width), lambda i: (i, 0)),
        scratch_shapes=[pltpu.VMEM((tm, width), BF16)],
        compiler_params=_cparams(("parallel",)),
        name="gmlp",
    )(h, h, ln_g, ln_b, wc, bs_exp, w_out, b_gate_a)


def _ssd_kernel(x_ref, z_ref, xbc_ref, wdt_ref, wdtT_ref, dtb_row_ref, dtb_col_ref,
                alog_row_ref, alog_col_ref, convw_ref, convb_ref, dexp_ref, normw_ref,
                e64_ref, e128_ref, ltri_ref, utri_ref,
                o_ref, xe_ref, st_ref, y_ref):
    inner = z_ref.shape[1]
    n_pairs = inner // LANES
    gs = SSD_GROUPS * SSD_STATE

    @pl.when(pl.program_id(1) == 0)
    def _():
        xe_ref[0:SUBLANES, :] = jnp.zeros((SUBLANES, xe_ref.shape[1]), F32)
        st_ref[...] = jnp.zeros(st_ref.shape, F32)

    xb = x_ref[...].astype(BF16)
    dt = jax.nn.softplus(jnp.dot(xb, wdt_ref[...], preferred_element_type=F32) + dtb_row_ref[...])
    dt_t = jax.nn.softplus(lax.dot_general(wdtT_ref[...], xb, NT_DIMS, preferred_element_type=F32)
                           + dtb_col_ref[...])
    da = dt * (-jnp.exp(alog_row_ref[...]))
    da_t = dt_t * (-jnp.exp(alog_col_ref[...]))
    da_parts = _split3(da)
    ltri = ltri_ref[...]
    acum = sum(jnp.dot(ltri, p, preferred_element_type=F32) for p in da_parts)
    utri = utri_ref[...]
    acum_t = sum(jnp.dot(p, utri, preferred_element_type=F32) for p in _split3(da_t))

    e64 = e64_ref[...]
    e128 = e128_ref[...]
    acum_parts = _split3(acum)
    dt_e = sum(jnp.dot(p, e64, preferred_element_type=F32) for p in _split3(dt))
    acum_e = sum(jnp.dot(p, e64, preferred_element_type=F32) for p in acum_parts)
    acum_b = sum(jnp.dot(p, e128, preferred_element_type=F32) for p in acum_parts)
    alast_e = acum_e[CHUNK - 1:CHUNK, :]

    xe_ref[SUBLANES:SUBLANES + CHUNK, :] = xbc_ref[...]
    conv = convb_ref[...]
    for k in range(SSD_CONV):
        off = SUBLANES - (SSD_CONV - 1) + k
        conv = conv + convw_ref[k:k + 1, :] * xe_ref[off:off + CHUNK, :]
    xe_ref[0:SUBLANES, :] = xe_ref[CHUNK:CHUNK + SUBLANES, :]
    xbc = jax.nn.silu(conv)

    xs = xbc[:, :inner]
    xdt = xs * dt_e
    xdt_b = xdt.astype(BF16)
    xw_b = (xdt * jnp.exp(alast_e - acum_e)).astype(BF16)
    ea = jnp.exp(acum_e)
    dec_state = jnp.exp(alast_e)
    dskip = dexp_ref[...]

    lane = lax.broadcasted_iota(jnp.int32, (CHUNK, LANES), 1)
    row = lax.broadcasted_iota(jnp.int32, (CHUNK, LANES), 0)
    causal = row >= lane
    first_half = lane < SSD_HEAD_DIM
    pairs_per_group = n_pairs // SSD_GROUPS

    for g in range(SSD_GROUPS):
        bm_f = xbc[:, inner + g * SSD_STATE: inner + (g + 1) * SSD_STATE]
        bm_g = bm_f.astype(BF16)
        bmt_g = bm_f.T.astype(BF16)
        cm_g = xbc[:, inner + gs + g * SSD_STATE: inner + gs + (g + 1) * SSD_STATE].astype(BF16)
        cb = lax.dot_general(cm_g, bm_g, NT_DIMS, preferred_element_type=F32)
        for j in range(pairs_per_group):
            pr = g * pairs_per_group + j
            cols = slice(pr * LANES, (pr + 1) * LANES)
            xdt_p = xdt_b[:, cols]
            y_p = None
            for k in range(2):
                hd = pr * 2 + k
                seg = acum_b[:, hd * LANES:(hd + 1) * LANES] - acum_t[hd:hd + 1, :]
                dec = jnp.exp(jnp.where(causal, seg, -jnp.inf))
                sc = (cb * dec).astype(BF16)
                keep = first_half if k == 0 else jnp.logical_not(first_half)
                rhs = jnp.where(keep, xdt_p, jnp.zeros_like(xdt_p))
                yk = jnp.dot(sc, rhs, preferred_element_type=F32)
                y_p = yk if y_p is None else y_p + yk
            st = st_ref[pr]
            y_off = jnp.dot(cm_g, st.astype(BF16), preferred_element_type=F32) * ea[:, cols]
            y_ref[:, cols] = y_p + y_off + xs[:, cols] * dskip[:, cols]
            upd = jnp.dot(bmt_g, xw_b[:, cols], preferred_element_type=F32)
            st_ref[pr] = dec_state[:, cols] * st + upd

    yf = y_ref[...] * jax.nn.silu(z_ref[...])
    ms = jnp.mean(yf * yf, axis=-1, keepdims=True)
    o_ref[...] = yf * lax.rsqrt(ms + LN_EPS) * normw_ref[...]


def _ssd(x2d, h, bsz, consts, z_col_block, xbc_col_block, inner, conv_dim):
    t, d = x2d.shape
    nc = t // bsz // CHUNK
    n_pairs = inner // LANES
    row_map = lambda b, c: (b * nc + c, 0)
    full = lambda a: pl.BlockSpec(a.shape, lambda b, c: (0,) * a.ndim)
    return pl.pallas_call(
        _ssd_kernel,
        out_shape=jax.ShapeDtypeStruct((t, inner), F32),
        grid=(bsz, nc),
        in_specs=[pl.BlockSpec((CHUNK, d), row_map),
                  pl.BlockSpec((CHUNK, inner), lambda b, c: (b * nc + c, z_col_block)),
                  pl.BlockSpec((CHUNK, conv_dim), lambda b, c: (b * nc + c, xbc_col_block))]
                 + [full(a) for a in consts],
        out_specs=pl.BlockSpec((CHUNK, inner), row_map),
        scratch_shapes=[pltpu.VMEM((CHUNK + SUBLANES, conv_dim), F32),
                        pltpu.VMEM((n_pairs, SSD_STATE, LANES), F32),
                        pltpu.VMEM((CHUNK, inner), F32)],
        compiler_params=_cparams(("parallel", "arbitrary")),
        name="ssd",
    )(x2d, h, h, *consts)


def _merge_kernel(alpha, ya_ref, yn_ref, gp_ref, bg_ref, wssd_ref, wo_ref, x_ref, lng_ref, lnb_ref,
                  o_ref):
    yb = jnp.dot(yn_ref[...].astype(BF16), wssd_ref[...], preferred_element_type=F32)
    merged = ya_ref[...] + jax.nn.sigmoid(gp_ref[...] + bg_ref[...]) * yb
    t = alpha * x_ref[...] + jnp.dot(merged.astype(BF16), wo_ref[...], preferred_element_type=F32)
    o_ref[...] = _layer_norm(t, lng_ref[...], lnb_ref[...])


def _merge(alpha, ya_g, y_norm, h, b_gate_b, w_ssd, w_o, x2d, ln_g, ln_b, gate_col_block):
    t, d = x2d.shape
    inner = y_norm.shape[1]
    tm = min(256, t)
    full = lambda a: pl.BlockSpec(a.shape, lambda i: (0,) * a.ndim)
    return pl.pallas_call(
        functools.partial(_merge_kernel, alpha),
        out_shape=jax.ShapeDtypeStruct((t, d), F32),
        grid=(t // tm,),
        in_specs=[pl.BlockSpec((tm, d), lambda i: (i, 0)),
                  pl.BlockSpec((tm, inner), lambda i: (i, 0)),
                  pl.BlockSpec((tm, d), lambda i: (i, gate_col_block)),
                  full(b_gate_b), full(w_ssd), full(w_o),
                  pl.BlockSpec((tm, d), lambda i: (i, 0)),
                  full(ln_g), full(ln_b)],
        out_specs=pl.BlockSpec((tm, d), lambda i: (i, 0)),
        compiler_params=_cparams(("parallel",)),
        name="merge",
    )(ya_g, y_norm, h, b_gate_b, w_ssd, w_o, x2d, ln_g, ln_b)


def _bitonic_merge_desc(xs):
    xs = list(xs)
    n = len(xs)
    j = n // 2
    while j >= 1:
        for i in range(n):
            l = i ^ j
            if l > i:
                xs[i], xs[l] = jnp.maximum(xs[i], xs[l]), jnp.minimum(xs[i], xs[l])
        j //= 2
    return xs


def _bitonic_sort_desc(xs):
    xs = list(xs)
    n = len(xs)
    k = 2
    while k <= n:
        j = k // 2
        while j >= 1:
            for i in range(n):
                l = i ^ j
                if l > i:
                    hi, lo = jnp.maximum(xs[i], xs[l]), jnp.minimum(xs[i], xs[l])
                    xs[i], xs[l] = (hi, lo) if (i & k) == 0 else (lo, hi)
            j //= 2
        k *= 2
    return xs


def _top16_and_next(s):
    nblk = s.shape[0] // SUBLANES
    blocks = [s[i * SUBLANES:(i + 1) * SUBLANES, :] for i in range(nblk)]
    top = _bitonic_sort_desc(blocks)
    for shift in (4, 2, 1):
        other = [pltpu.roll(r, shift, axis=0) for r in top]
        top = _bitonic_merge_desc([jnp.maximum(top[i], other[nblk - 1 - i]) for i in range(nblk)])
    kth = top[PEER_TOPK - 1]
    nxt = None
    for blk in blocks:
        cand = jnp.where(blk < kth, blk, -jnp.inf)
        nxt = cand if nxt is None else jnp.maximum(nxt, cand)
    for shift in (4, 2, 1):
        nxt = jnp.maximum(nxt, pltpu.roll(nxt, shift, axis=0))
    return top + [nxt]


def _peer_prep_kernel(alpha, x1_ref, p_ref, wqT_ref, k1_ref, k2_ref, wg_ref, bg_ref, wp_ref,
                      base_ref, xb_ref, nsel_ref, cw_ref, r2_ref, e2_ref,
                      qT_ref, s1_ref, s2_ref, top1_ref, top2_ref, row_ref):
    tm = x1_ref.shape[0]
    x1 = x1_ref[...]
    xb = x1.astype(BF16)
    xb_ref[...] = xb
    gate = jax.nn.sigmoid(jnp.dot(xb, wg_ref[...], preferred_element_type=F32) + bg_ref[...])
    ple = gate * jnp.dot(p_ref[...].astype(BF16), wp_ref[...], preferred_element_type=F32)
    base_ref[...] = alpha * x1 + ple

    qT_ref[...] = lax.dot_general(wqT_ref[...], xb, NT_DIMS, preferred_element_type=F32).astype(BF16)
    top1_ref[...] = jnp.zeros(top1_ref.shape, F32)
    top2_ref[...] = jnp.zeros(top2_ref.shape, F32)
    sub = lax.broadcasted_iota(jnp.int32, (SUBLANES, tm), 0)
    dkey = 2 * PEER_NKEYS

    def head_scores(hd, carry):
        q1 = qT_ref[pl.ds(pl.multiple_of(hd * dkey, dkey), PEER_NKEYS), :]
        q2 = qT_ref[pl.ds(pl.multiple_of(hd * dkey + PEER_NKEYS, PEER_NKEYS), PEER_NKEYS), :]
        s1 = jnp.dot(k1_ref[hd], q1, preferred_element_type=F32)
        s2 = jnp.dot(k2_ref[hd], q2, preferred_element_type=F32)
        s1_ref[hd] = s1
        s2_ref[hd] = s2
        mine = sub == hd
        for i, v in enumerate(_top16_and_next(s1)):
            top1_ref[i] = jnp.where(mine, v, top1_ref[i])
        for i, v in enumerate(_top16_and_next(s2)):
            top2_ref[i] = jnp.where(mine, v, top2_ref[i])
        return carry

    lax.fori_loop(0, PEER_HEADS, head_scores, 0)

    nt = PEER_TOPK + 1
    t1 = [top1_ref[i] for i in range(nt)]
    t2 = [top2_ref[i] for i in range(nt)]
    cands = [t1[a] + t2[b] for a in range(nt) for b in range(nt) if (a + 1) * (b + 1) <= nt]
    cands += [jnp.full((SUBLANES, tm), -jnp.inf, F32)] * (64 - len(cands))
    srt = _bitonic_sort_desc(cands)
    zsum = None
    for i in range(PEER_TOPK):
        e = jnp.exp(srt[i] - srt[0])
        zsum = e if zsum is None else zsum + e
    row_ref[0] = 0.5 * (srt[PEER_TOPK - 1] + srt[PEER_TOPK])
    row_ref[1] = t1[0]
    row_ref[2] = t2[0]
    row_ref[3] = 1.0 / zsum

    def head_factors(hd, carry):
        theta = row_ref[0, pl.ds(hd, 1), :]
        m1 = row_ref[1, pl.ds(hd, 1), :]
        m2 = row_ref[2, pl.ds(hd, 1), :]
        inv_z = row_ref[3, pl.ds(hd, 1), :]
        s1 = s1_ref[hd]
        s2 = s2_ref[hd]
        thr = theta - s1
        nsel = jnp.zeros_like(s1)
        rank2 = jnp.zeros_like(s2)
        for k in range(PEER_TOPK):
            v2k = top2_ref[k, pl.ds(hd, 1), :]
            nsel = nsel + jnp.where(v2k >= thr, 1.0, 0.0)
            rank2 = rank2 + jnp.where(v2k > s2, 1.0, 0.0)
        nsel_ref[hd] = nsel
        key_rows = pl.ds(pl.multiple_of(hd * PACKED_KEYS, PACKED_KEYS), PACKED_KEYS)
        r2_ref[key_rows, :] = pltpu.bitcast(rank2.astype(BF16), jnp.uint32)
        cw_ref[hd] = jnp.exp(s1 - m1) * inv_z
        e2_ref[key_rows, :] = pltpu.bitcast(jnp.exp(s2 - m2).astype(BF16), jnp.uint32)
        return carry

    lax.fori_loop(0, PEER_HEADS, head_factors, 0)


def _peer_prep(alpha, x1, p2d, wqT, k1, k2, wg, bg, wp):
    t, d = x1.shape
    tm = min(256, t)
    full = lambda a: pl.BlockSpec(a.shape, lambda i: (0,) * a.ndim)
    head_shape = jax.ShapeDtypeStruct((PEER_HEADS, PEER_NKEYS, t), F32)
    flat_bf16 = jax.ShapeDtypeStruct((PEER_HEADS * PACKED_KEYS, t), jnp.uint32)
    flat_spec = pl.BlockSpec((PEER_HEADS * PACKED_KEYS, tm), lambda i: (0, i))
    head_spec = pl.BlockSpec((PEER_HEADS, PEER_NKEYS, tm), lambda i: (0, 0, i))
    nt = PEER_TOPK + 1
    return pl.pallas_call(
        functools.partial(_peer_prep_kernel, alpha),
        out_shape=(jax.ShapeDtypeStruct((t, d), F32), jax.ShapeDtypeStruct((t, d), BF16),
                   head_shape, head_shape, flat_bf16, flat_bf16),
        grid=(t // tm,),
        in_specs=[pl.BlockSpec((tm, d), lambda i: (i, 0)),
                  pl.BlockSpec((tm, p2d.shape[1]), lambda i: (i, 0)),
                  full(wqT), full(k1), full(k2), full(wg), full(bg), full(wp)],
        out_specs=(pl.BlockSpec((tm, d), lambda i: (i, 0)), pl.BlockSpec((tm, d), lambda i: (i, 0)),
                   head_spec, head_spec, flat_spec, flat_spec),
        scratch_shapes=[pltpu.VMEM((wqT.shape[0], tm), BF16),
                        pltpu.VMEM((PEER_HEADS, PEER_NKEYS, tm), F32),
                        pltpu.VMEM((PEER_HEADS, PEER_NKEYS, tm), F32),
                        pltpu.VMEM((nt, SUBLANES, tm), F32),
                        pltpu.VMEM((nt, SUBLANES, tm), F32),
                        pltpu.VMEM((4, SUBLANES, tm), F32)],
        compiler_params=_cparams(("parallel",)),
        name="peer_prep",
    )(x1, p2d, wqT, k1, k2, wg, bg, wp)


def _peer_dense_kernel(xb_ref, base_ref, nsel_ref, cw_ref, r2_ref, e2_ref, u_ref, vt_ref, vtp_ref,
                       lng_ref, lnb_ref, o_ref, acc_ref, ma_ref, mb_ref):
    j = pl.program_id(1)
    half, tb = ma_ref.shape
    d = acc_ref.shape[0]
    tile = (PEER_NKEYS, LANES)
    piece = 2
    piece_rows = piece * PEER_NKEYS
    pieces_per_chunk = half // piece_rows

    @pl.when(j == 0)
    def _():
        acc_ref[...] = jnp.zeros(acc_ref.shape, F32)
        mb_ref[...] = jnp.zeros(mb_ref.shape, BF16)

    e1_rows = pl.ds(pl.multiple_of(j * SUBLANES, SUBLANES), SUBLANES)

    def scores(p):
        rows = slice(p * piece_rows, (p + 1) * piece_rows)
        return lax.dot_general(u_ref[rows, :], xb_ref[...], NT_DIMS, preferred_element_type=F32)

    def weigh(p, ht, m_ref):
        base_row = (p % pieces_per_chunk) * piece_rows
        for c in range(tb // LANES):
            cols = slice(c * LANES, (c + 1) * LANES)
            w = [None] * piece
            for hd in range(PEER_HEADS):
                rank2 = pltpu.bitcast(r2_ref[hd * PACKED_KEYS:(hd + 1) * PACKED_KEYS, cols], BF16)
                ef = pltpu.bitcast(e2_ref[hd * PACKED_KEYS:(hd + 1) * PACKED_KEYS, cols], BF16)
                nsel8 = nsel_ref[hd, e1_rows, cols]
                cw8 = cw_ref[hd, e1_rows, cols]
                for i in range(piece):
                    r = p * piece + i
                    nsel = jnp.broadcast_to(nsel8[r:r + 1, :], tile).astype(BF16)
                    cw = jnp.broadcast_to(cw8[r:r + 1, :], tile).astype(BF16)
                    term = jnp.where(rank2 < nsel, ef * cw, jnp.zeros_like(ef))
                    w[i] = term if w[i] is None else w[i] + term
            for i in range(piece):
                act = jax.nn.gelu(ht[i * PEER_NKEYS:(i + 1) * PEER_NKEYS, cols].astype(BF16))
                rows = slice(base_row + i * PEER_NKEYS, base_row + (i + 1) * PEER_NKEYS)
                m_ref[rows, cols] = act * w[i]

    n_proj = 2

    def project(vt_cols, m_ref, r):
        rows = slice(r * d // n_proj, (r + 1) * d // n_proj)
        acc_ref[rows, :] += jnp.dot(vt_cols(rows), m_ref[...], preferred_element_type=F32)

    vt_prev_b = lambda rows: vtp_ref[rows, :]
    vt_a = lambda rows: vt_ref[rows, 0:half]
    vt_b = lambda rows: vt_ref[rows, half:2 * half]

    h0 = scores(0)
    h1 = scores(1)
    weigh(0, h0, ma_ref)
    project(vt_prev_b, mb_ref, 0)
    h2 = scores(2)
    weigh(1, h1, ma_ref)
    project(vt_prev_b, mb_ref, 1)
    h3 = scores(3)
    weigh(2, h2, mb_ref)
    project(vt_a, ma_ref, 0)
    project(vt_a, ma_ref, 1)
    weigh(3, h3, mb_ref)

    @pl.when(j == pl.num_programs(1) - 1)
    def _():
        for r in range(n_proj):
            project(vt_b, mb_ref, r)
        y = base_ref[...] + acc_ref[...].T
        o_ref[...] = _layer_norm(y, lng_ref[...], lnb_ref[...])


def _peer_dense(xb, base, nsel, cw, r2, e2, u_bf, vt_bf, ln_g, ln_b):
    t, d = base.shape
    ne = u_bf.shape[0]
    tb = min(512, t)
    ec = SUBLANES * PEER_NKEYS
    full = lambda a: pl.BlockSpec(a.shape, lambda i, j: (0,) * a.ndim)
    head_spec = pl.BlockSpec((PEER_HEADS, PEER_NKEYS, tb), lambda i, j: (0, 0, i))
    flat_spec = pl.BlockSpec((PEER_HEADS * PACKED_KEYS, tb), lambda i, j: (0, i))
    return pl.pallas_call(
        _peer_dense_kernel,
        out_shape=jax.ShapeDtypeStruct((t, d), F32),
        grid=(t // tb, ne // ec),
        in_specs=[pl.BlockSpec((tb, d), lambda i, j: (i, 0)),
                  pl.BlockSpec((tb, d), lambda i, j: (i, 0)),
                  head_spec, head_spec, flat_spec, flat_spec,
                  pl.BlockSpec((ec, d), lambda i, j: (j, 0)),
                  pl.BlockSpec((d, ec), lambda i, j: (0, j)),
                  pl.BlockSpec((d, ec // 2), lambda i, j: (0, jnp.maximum(2 * j - 1, 0))),
                  full(ln_g), full(ln_b)],
        out_specs=pl.BlockSpec((tb, d), lambda i, j: (i, 0)),
        scratch_shapes=[pltpu.VMEM((d, tb), F32), pltpu.VMEM((ec // 2, tb), BF16),
                        pltpu.VMEM((ec // 2, tb), BF16)],
        compiler_params=_cparams(("parallel", "arbitrary")),
        name="peer_dense",
    )(xb, base, nsel, cw, r2, e2, u_bf, vt_bf, vt_bf, ln_g, ln_b)


def _pad_cols(a, n):
    return jnp.pad(a, ((0, 0), (0, n - a.shape[1])))


def _layer(x2d, p2d, bsz, alpha, w_in, b_gate, gm_ln_g, gm_ln_b, gm_w_s, gm_b_s, gm_w_out,
           ssd_conv_w, ssd_conv_b, ssd_dt_bias, ssd_a_log, ssd_d, ssd_norm_w, ssd_w_out,
           w_o, ln1_g, ln1_b, peer_w_q, peer_k1, peer_k2, peer_u, peer_v,
           ple_w_proj, ple_w_gate, ple_b_gate, ln2_g, ln2_b):
    d = x2d.shape[1]
    width = gm_ln_g.shape[0]
    inner = ssd_norm_w.shape[0]
    conv_dim = ssd_conv_b.shape[0]
    heads = ssd_a_log.shape[0]
    o1 = 2 * width
    o2 = o1 + inner
    o3 = o2 + conv_dim
    o4 = o3 + heads
    row = lambda v: v.reshape(1, -1)

    w_main = jnp.concatenate([w_in[:, :o3], w_in[:, o4:]], axis=1).astype(BF16)
    h = _inproj(x2d, w_main)
    gate_a_block = o3 // d
    gate_b_block = gate_a_block + 1

    causal = jnp.tril(jnp.ones((CHUNK, CHUNK), dtype=bool))
    wc = jnp.where(causal[None], gm_w_s, 0.0).astype(BF16)
    bs_exp = jnp.repeat(gm_b_s.T, width // GM_GROUPS, axis=1)
    ya_g = _gmlp(h, row(gm_ln_g), row(gm_ln_b), wc, bs_exp, gm_w_out.astype(BF16),
                 row(b_gate[:d]), gate_a_block)

    w_dt = _pad_cols(w_in[:, o3:o4], LANES)
    pad_row = lambda v: _pad_cols(row(v), LANES)
    hid = jnp.arange(LANES)[:, None]
    e64 = (hid == (jnp.arange(inner)[None, :] // SSD_HEAD_DIM)).astype(BF16)
    e128 = (hid == (jnp.arange(heads * LANES)[None, :] // LANES)).astype(BF16)
    ltri = causal.astype(BF16)
    consts = (w_dt.astype(BF16), w_dt.T.astype(BF16), pad_row(ssd_dt_bias), pad_row(ssd_dt_bias).T,
              pad_row(ssd_a_log), pad_row(ssd_a_log).T, ssd_conv_w, row(ssd_conv_b),
              row(jnp.repeat(ssd_d, SSD_HEAD_DIM)), row(ssd_norm_w), e64, e128, ltri, ltri.T)
    y_norm = _ssd(x2d, h, bsz, consts, o1 // inner, o2 // conv_dim, inner, conv_dim)

    x1 = _merge(alpha, ya_g, y_norm, h, row(b_gate[d:]), ssd_w_out.astype(BF16), w_o.astype(BF16),
                x2d, row(ln1_g), row(ln1_b), gate_b_block)

    base, xb, nsel, cw, r2, e2 = _peer_prep(
        alpha, x1, p2d, peer_w_q.T.astype(BF16), peer_k1.astype(BF16), peer_k2.astype(BF16),
        ple_w_gate.astype(BF16), row(ple_b_gate), ple_w_proj.astype(BF16))
    return _peer_dense(xb, base, nsel, cw, r2, e2, peer_u.astype(BF16), peer_v.T.astype(BF16),
                       row(ln2_g), row(ln2_b))


def kernel(x, p, w_in, b_gate, gm_ln_g, gm_ln_b, gm_w_s, gm_b_s, gm_w_out, ssd_conv_w, ssd_conv_b,
           ssd_dt_bias, ssd_a_log, ssd_d, ssd_norm_w, ssd_w_out, w_o, ln1_g, ln1_b, peer_w_q,
           peer_k1, peer_k2, peer_u, peer_v, ple_w_proj, ple_w_gate, ple_b_gate, ln2_g, ln2_b):
    bsz, s, d = x.shape
    depth = w_in.shape[0]
    alpha = (2 * depth) ** 0.25
    x2d = x.reshape(bsz * s, d)
    for i in range(depth):
        x2d = _layer(x2d, p[i].reshape(bsz * s, -1), bsz, alpha, w_in[i], b_gate[i], gm_ln_g[i],
                     gm_ln_b[i], gm_w_s[i], gm_b_s[i], gm_w_out[i], ssd_conv_w[i], ssd_conv_b[i],
                     ssd_dt_bias[i], ssd_a_log[i], ssd_d[i], ssd_norm_w[i], ssd_w_out[i], w_o[i],
                     ln1_g[i], ln1_b[i], peer_w_q[i], peer_k1[i], peer_k2[i], peer_u[i], peer_v[i],
                     ple_w_proj[i], ple_w_gate[i], ple_b_gate[i], ln2_g[i], ln2_b[i])
    return x2d.reshape(bsz, s, d)
```

```python
import functools
import math

import jax
import jax.numpy as jnp
from jax import lax
from jax.experimental import pallas as pl
from jax.experimental.pallas import tpu as pltpu

BF16 = jnp.bfloat16
F32 = jnp.float32

LN_EPS = 1e-5
CHUNK = 128
GM_GROUPS = 8
SSD_HEAD_DIM = 64
SSD_STATE = 128
SSD_GROUPS = 8
SSD_CONV = 4
PEER_HEADS = 8
PEER_NKEYS = 128
PEER_TOPK = 16
PACKED_KEYS = PEER_NKEYS // 2
LANES = 128
SUBLANES = 8
MXU_COLS = 256
VMEM_LIMIT = 60 * 1024 * 1024

NT_DIMS = (((1,), (1,)), ((), ()))


def _cparams(sem):
    return pltpu.CompilerParams(dimension_semantics=sem, vmem_limit_bytes=VMEM_LIMIT)


def _layer_norm(t, g, b):
    mu = jnp.mean(t, axis=-1, keepdims=True)
    d = t - mu
    var = jnp.mean(d * d, axis=-1, keepdims=True)
    return d * lax.rsqrt(var + LN_EPS) * g + b


def _split3(v):
    hi = v.astype(BF16)
    r = v - hi.astype(F32)
    mid = r.astype(BF16)
    lo = (r - mid.astype(F32)).astype(BF16)
    return hi, mid, lo


def _inproj_kernel(x_ref, w_ref, o_ref, xb_ref):
    @pl.when(pl.program_id(1) == 0)
    def _():
        xb_ref[...] = x_ref[...].astype(BF16)

    o_ref[...] = jnp.dot(xb_ref[...], w_ref[...], preferred_element_type=F32).astype(o_ref.dtype)


def _inproj(x2d, w):
    t, d = x2d.shape
    n = w.shape[1]
    tm = min(1024, t)
    tn = 2048
    return pl.pallas_call(
        _inproj_kernel,
        out_shape=jax.ShapeDtypeStruct((t, n), BF16),
        grid=(t // tm, n // tn),
        in_specs=[pl.BlockSpec((tm, d), lambda i, j: (i, 0)),
                  pl.BlockSpec((d, tn), lambda i, j: (0, j))],
        out_specs=pl.BlockSpec((tm, tn), lambda i, j: (i, j)),
        scratch_shapes=[pltpu.VMEM((tm, d), BF16)],
        compiler_params=_cparams(("parallel", "arbitrary")),
        name="inproj",
    )(x2d, w)


def _gmlp_kernel(uv_ref, gp_ref, lng_ref, lnb_ref, wc_ref, bs_ref, wout_ref, bg_ref,
                 o_ref, pre_ref):
    width = lng_ref.shape[1]
    gd = width // GM_GROUPS
    for c in range(uv_ref.shape[0] // CHUNK):
        rows = slice(c * CHUNK, (c + 1) * CHUNK)
        g = jax.nn.gelu(uv_ref[rows, :].astype(F32))
        u = g[:, :width]
        v = _layer_norm(g[:, width:], lng_ref[...], lnb_ref[...]).astype(BF16)
        for grp in range(GM_GROUPS):
            cols = slice(grp * gd, (grp + 1) * gd)
            mixed = jnp.dot(wc_ref[grp], v[:, cols], preferred_element_type=F32)
            pre_ref[rows, cols] = (u[:, cols] * (mixed + bs_ref[:, cols])).astype(BF16)
    ya = jnp.dot(pre_ref[...], wout_ref[...], preferred_element_type=F32)
    o_ref[...] = jax.nn.sigmoid(gp_ref[...] + bg_ref[...]) * ya


def _gmlp(h, ln_g, ln_b, wc, bs_exp, w_out, b_gate_a, gate_col_block):
    t = h.shape[0]
    width = ln_g.shape[1]
    tm = min(512, t)
    full = lambda shape: pl.BlockSpec(shape, lambda i: (0,) * len(shape))
    return pl.pallas_call(
        _gmlp_kernel,
        out_shape=jax.ShapeDtypeStruct((t, width), F32),
        grid=(t // tm,),
        in_specs=[pl.BlockSpec((tm, 2 * width), lambda i: (i, 0)),
                  pl.BlockSpec((tm, width), lambda i: (i, gate_col_block)),
                  full(ln_g.shape), full(ln_b.shape), full(wc.shape), full(bs_exp.shape),
                  full(w_out.shape), full(b_gate_a.shape)],
        out_specs=pl.BlockSpec((tm, width), lambda i: (i, 0)),
        scratch_shapes=[pltpu.VMEM((tm, width), BF16)],
        compiler_params=_cparams(("parallel",)),
        name="gmlp",
    )(h, h, ln_g, ln_b, wc, bs_exp, w_out, b_gate_a)


def _ssd_kernel(x_ref, z_ref, xbc_ref, wdt_ref, wdtT_ref, dtb_row_ref, dtb_col_ref,
                alog_row_ref, alog_col_ref, convw_ref, convb_ref, dexp_ref, normw_ref,
                e64_ref, e128_ref, ltri_ref, utri_ref,
                o_ref, xe_ref, st_ref, y_ref):
    inner = z_ref.shape[1]
    n_pairs = inner // LANES
    gs = SSD_GROUPS * SSD_STATE

    @pl.when(pl.program_id(1) == 0)
    def _():
        xe_ref[0:SUBLANES, :] = jnp.zeros((SUBLANES, xe_ref.shape[1]), F32)
        st_ref[...] = jnp.zeros(st_ref.shape, F32)

    xb = x_ref[...].astype(BF16)
    dt = jax.nn.softplus(jnp.dot(xb, wdt_ref[...], preferred_element_type=F32) + dtb_row_ref[...])
    dt_t = jax.nn.softplus(lax.dot_general(wdtT_ref[...], xb, NT_DIMS, preferred_element_type=F32)
                           + dtb_col_ref[...])
    da = dt * (-jnp.exp(alog_row_ref[...]))
    da_t = dt_t * (-jnp.exp(alog_col_ref[...]))
    da_parts = _split3(da)
    ltri = ltri_ref[...]
    acum = sum(jnp.dot(ltri, p, preferred_element_type=F32) for p in da_parts)
    utri = utri_ref[...]
    acum_t = sum(jnp.dot(p, utri, preferred_element_type=F32) for p in _split3(da_t))

    e64 = e64_ref[...]
    e128 = e128_ref[...]
    acum_parts = _split3(acum)
    dt_e = sum(jnp.dot(p, e64, preferred_element_type=F32) for p in _split3(dt))
    acum_e = sum(jnp.dot(p, e64, preferred_element_type=F32) for p in acum_parts)
    acum_b = sum(jnp.dot(p, e128, preferred_element_type=F32) for p in acum_parts)
    alast_e = acum_e[CHUNK - 1:CHUNK, :]

    xe_ref[SUBLANES:SUBLANES + CHUNK, :] = xbc_ref[...].astype(F32)
    conv = convb_ref[...]
    for k in range(SSD_CONV):
        off = SUBLANES - (SSD_CONV - 1) + k
        conv = conv + convw_ref[k:k + 1, :] * xe_ref[off:off + CHUNK, :]
    xe_ref[0:SUBLANES, :] = xe_ref[CHUNK:CHUNK + SUBLANES, :]
    xbc = jax.nn.silu(conv)

    xs = xbc[:, :inner]
    xdt = xs * dt_e
    xdt_b = xdt.astype(BF16)
    xw_b = (xdt * jnp.exp(alast_e - acum_e)).astype(BF16)
    ea = jnp.exp(acum_e)
    dec_state = jnp.exp(alast_e)
    dskip = dexp_ref[...]

    lane = lax.broadcasted_iota(jnp.int32, (CHUNK, LANES), 1)
    row = lax.broadcasted_iota(jnp.int32, (CHUNK, LANES), 0)
    causal = row >= lane
    first_half = lane < SSD_HEAD_DIM
    pairs_per_group = n_pairs // SSD_GROUPS

    for g in range(SSD_GROUPS):
        bm_f = xbc[:, inner + g * SSD_STATE: inner + (g + 1) * SSD_STATE]
        bm_g = bm_f.astype(BF16)
        bmt_g = bm_f.T.astype(BF16)
        cm_g = xbc[:, inner + gs + g * SSD_STATE: inner + gs + (g + 1) * SSD_STATE].astype(BF16)
        cb = lax.dot_general(cm_g, bm_g, NT_DIMS, preferred_element_type=F32)
        for j in range(pairs_per_group):
            pr = g * pairs_per_group + j
            cols = slice(pr * LANES, (pr + 1) * LANES)
            xdt_p = xdt_b[:, cols]
            y_p = None
            for k in range(2):
                hd = pr * 2 + k
                seg = acum_b[:, hd * LANES:(hd + 1) * LANES] - acum_t[hd:hd + 1, :]
                dec = jnp.exp(jnp.where(causal, seg, -jnp.inf))
                sc = (cb * dec).astype(BF16)
                keep = first_half if k == 0 else jnp.logical_not(first_half)
                rhs = jnp.where(keep, xdt_p, jnp.zeros_like(xdt_p))
                yk = jnp.dot(sc, rhs, preferred_element_type=F32)
                y_p = yk if y_p is None else y_p + yk
            st = st_ref[pr]
            y_off = jnp.dot(cm_g, st.astype(BF16), preferred_element_type=F32) * ea[:, cols]
            y_ref[:, cols] = y_p + y_off + xs[:, cols] * dskip[:, cols]
            upd = jnp.dot(bmt_g, xw_b[:, cols], preferred_element_type=F32)
            st_ref[pr] = dec_state[:, cols] * st + upd

    yf = y_ref[...] * jax.nn.silu(z_ref[...].astype(F32))
    ms = jnp.mean(yf * yf, axis=-1, keepdims=True)
    o_ref[...] = yf * lax.rsqrt(ms + LN_EPS) * normw_ref[...]


def _ssd(x2d, h, bsz, consts, z_col_block, xbc_col_block, inner, conv_dim):
    t, d = x2d.shape
    nc = t // bsz // CHUNK
    n_pairs = inner // LANES
    row_map = lambda b, c: (b * nc + c, 0)
    full = lambda a: pl.BlockSpec(a.shape, lambda b, c: (0,) * a.ndim)
    return pl.pallas_call(
        _ssd_kernel,
        out_shape=jax.ShapeDtypeStruct((t, inner), F32),
        grid=(bsz, nc),
        in_specs=[pl.BlockSpec((CHUNK, d), row_map),
                  pl.BlockSpec((CHUNK, inner), lambda b, c: (b * nc + c, z_col_block)),
                  pl.BlockSpec((CHUNK, conv_dim), lambda b, c: (b * nc + c, xbc_col_block))]
                 + [full(a) for a in consts],
        out_specs=pl.BlockSpec((CHUNK, inner), row_map),
        scratch_shapes=[pltpu.VMEM((CHUNK + SUBLANES, conv_dim), F32),
                        pltpu.VMEM((n_pairs, SSD_STATE, LANES), F32),
                        pltpu.VMEM((CHUNK, inner), F32)],
        compiler_params=_cparams(("parallel", "arbitrary")),
        name="ssd",
    )(x2d, h, h, *consts)


def _merge_kernel(alpha, ya_ref, yn_ref, gp_ref, bg_ref, wssd_ref, wo_ref, x_ref, lng_ref, lnb_ref,
                  o_ref):
    yb = jnp.dot(yn_ref[...].astype(BF16), wssd_ref[...], preferred_element_type=F32)
    merged = ya_ref[...] + jax.nn.sigmoid(gp_ref[...] + bg_ref[...]) * yb
    t = alpha * x_ref[...] + jnp.dot(merged.astype(BF16), wo_ref[...], preferred_element_type=F32)
    o_ref[...] = _layer_norm(t, lng_ref[...], lnb_ref[...])


def _merge(alpha, ya_g, y_norm, h, b_gate_b, w_ssd, w_o, x2d, ln_g, ln_b, gate_col_block):
    t, d = x2d.shape
    inner = y_norm.shape[1]
    tm = min(512, t)
    full = lambda a: pl.BlockSpec(a.shape, lambda i: (0,) * a.ndim)
    return pl.pallas_call(
        functools.partial(_merge_kernel, alpha),
        out_shape=jax.ShapeDtypeStruct((t, d), F32),
        grid=(t // tm,),
        in_specs=[pl.BlockSpec((tm, d), lambda i: (i, 0)),
                  pl.BlockSpec((tm, inner), lambda i: (i, 0)),
                  pl.BlockSpec((tm, d), lambda i: (i, gate_col_block)),
                  full(b_gate_b), full(w_ssd), full(w_o),
                  pl.BlockSpec((tm, d), lambda i: (i, 0)),
                  full(ln_g), full(ln_b)],
        out_specs=pl.BlockSpec((tm, d), lambda i: (i, 0)),
        compiler_params=_cparams(("parallel",)),
        name="merge",
    )(ya_g, y_norm, h, b_gate_b, w_ssd, w_o, x2d, ln_g, ln_b)


def _bitonic_merge_desc(xs):
    xs = list(xs)
    n = len(xs)
    j = n // 2
    while j >= 1:
        for i in range(n):
            l = i ^ j
            if l > i:
                xs[i], xs[l] = jnp.maximum(xs[i], xs[l]), jnp.minimum(xs[i], xs[l])
        j //= 2
    return xs


def _bitonic_sort_desc(xs):
    xs = list(xs)
    n = len(xs)
    k = 2
    while k <= n:
        j = k // 2
        while j >= 1:
            for i in range(n):
                l = i ^ j
                if l > i:
                    hi, lo = jnp.maximum(xs[i], xs[l]), jnp.minimum(xs[i], xs[l])
                    xs[i], xs[l] = (hi, lo) if (i & k) == 0 else (lo, hi)
            j //= 2
        k *= 2
    return xs


def _top16_and_next(s):
    nblk = s.shape[0] // SUBLANES
    blocks = [s[i * SUBLANES:(i + 1) * SUBLANES, :] for i in range(nblk)]
    top = _bitonic_sort_desc(blocks)
    for shift in (4, 2, 1):
        other = [pltpu.roll(r, shift, axis=0) for r in top]
        top = _bitonic_merge_desc([jnp.maximum(top[i], other[nblk - 1 - i]) for i in range(nblk)])
    kth = top[PEER_TOPK - 1]
    nxt = None
    for blk in blocks:
        cand = jnp.where(blk < kth, blk, -jnp.inf)
        nxt = cand if nxt is None else jnp.maximum(nxt, cand)
    for shift in (4, 2, 1):
        nxt = jnp.maximum(nxt, pltpu.roll(nxt, shift, axis=0))
    return top + [nxt]


def _peer_prep_kernel(alpha, x1_ref, p_ref, wqT_ref, k1_ref, k2_ref, wg_ref, bg_ref, wp_ref,
                      base_ref, xbt_ref, nsel_ref, cw_ref, r2_ref, e2_ref,
                      qT_ref, s1_ref, s2_ref, top1_ref, top2_ref, row_ref):
    tm = x1_ref.shape[0]
    x1 = x1_ref[...]
    xb = x1.astype(BF16)
    xbt = x1.T.astype(BF16)
    xbt_ref[...] = pltpu.bitcast(xbt, jnp.uint32)
    gate = jax.nn.sigmoid(jnp.dot(xb, wg_ref[...], preferred_element_type=F32) + bg_ref[...])
    ple = gate * jnp.dot(p_ref[...].astype(BF16), wp_ref[...], preferred_element_type=F32)
    base_ref[...] = alpha * x1 + ple

    qT_ref[...] = jnp.dot(wqT_ref[...], xbt, preferred_element_type=F32).astype(BF16)
    top1_ref[...] = jnp.zeros(top1_ref.shape, F32)
    top2_ref[...] = jnp.zeros(top2_ref.shape, F32)
    sub = lax.broadcasted_iota(jnp.int32, (SUBLANES, tm), 0)
    dkey = 2 * PEER_NKEYS

    def head_scores(hd, carry):
        q1 = qT_ref[pl.ds(pl.multiple_of(hd * dkey, dkey), PEER_NKEYS), :]
        q2 = qT_ref[pl.ds(pl.multiple_of(hd * dkey + PEER_NKEYS, PEER_NKEYS), PEER_NKEYS), :]
        s1 = jnp.dot(k1_ref[hd], q1, preferred_element_type=F32)
        s2 = jnp.dot(k2_ref[hd], q2, preferred_element_type=F32)
        s1_ref[hd] = s1
        s2_ref[hd] = s2
        mine = sub == hd
        for i, v in enumerate(_top16_and_next(s1)):
            top1_ref[i] = jnp.where(mine, v, top1_ref[i])
        for i, v in enumerate(_top16_and_next(s2)):
            top2_ref[i] = jnp.where(mine, v, top2_ref[i])
        return carry

    lax.fori_loop(0, PEER_HEADS, head_scores, 0)

    nt = PEER_TOPK + 1
    t1 = [top1_ref[i] for i in range(nt)]
    t2 = [top2_ref[i] for i in range(nt)]
    cands = [t1[a] + t2[b] for a in range(nt) for b in range(nt) if (a + 1) * (b + 1) <= nt]
    cands += [jnp.full((SUBLANES, tm), -jnp.inf, F32)] * (64 - len(cands))
    srt = _bitonic_sort_desc(cands)
    zsum = None
    for i in range(PEER_TOPK):
        e = jnp.exp(srt[i] - srt[0])
        zsum = e if zsum is None else zsum + e
    row_ref[0] = 0.5 * (srt[PEER_TOPK - 1] + srt[PEER_TOPK])
    row_ref[1] = t1[0]
    row_ref[2] = t2[0]
    row_ref[3] = 1.0 / zsum

    def head_factors(hd, carry):
        theta = row_ref[0, pl.ds(hd, 1), :]
        m1 = row_ref[1, pl.ds(hd, 1), :]
        m2 = row_ref[2, pl.ds(hd, 1), :]
        inv_z = row_ref[3, pl.ds(hd, 1), :]
        s1 = s1_ref[hd]
        s2 = s2_ref[hd]
        thr = theta - s1
        nsel = jnp.zeros_like(s1)
        rank2 = jnp.zeros_like(s2)
        for k in range(PEER_TOPK):
            v2k = top2_ref[k, pl.ds(hd, 1), :]
            nsel = nsel + jnp.where(v2k >= thr, 1.0, 0.0)
            rank2 = rank2 + jnp.where(v2k > s2, 1.0, 0.0)
        nsel_ref[hd] = nsel
        key_rows = pl.ds(pl.multiple_of(hd * PACKED_KEYS, PACKED_KEYS), PACKED_KEYS)
        r2_ref[key_rows, :] = pltpu.bitcast(rank2.astype(BF16), jnp.uint32)
        cw_ref[hd] = jnp.exp(s1 - m1) * inv_z
        e2_ref[key_rows, :] = pltpu.bitcast(jnp.exp(s2 - m2).astype(BF16), jnp.uint32)
        return carry

    lax.fori_loop(0, PEER_HEADS, head_factors, 0)


def _peer_prep(alpha, x1, p2d, wqT, k1, k2, wg, bg, wp):
    t, d = x1.shape
    tm = min(512, t)
    full = lambda a: pl.BlockSpec(a.shape, lambda i: (0,) * a.ndim)
    head_shape = jax.ShapeDtypeStruct((PEER_HEADS, PEER_NKEYS, t), F32)
    flat_bf16 = jax.ShapeDtypeStruct((PEER_HEADS * PACKED_KEYS, t), jnp.uint32)
    flat_spec = pl.BlockSpec((PEER_HEADS * PACKED_KEYS, tm), lambda i: (0, i))
    head_spec = pl.BlockSpec((PEER_HEADS, PEER_NKEYS, tm), lambda i: (0, 0, i))
    nt = PEER_TOPK + 1
    return pl.pallas_call(
        functools.partial(_peer_prep_kernel, alpha),
        out_shape=(jax.ShapeDtypeStruct((t, d), F32), jax.ShapeDtypeStruct((d // 2, t), jnp.uint32),
                   head_shape, head_shape, flat_bf16, flat_bf16),
        grid=(t // tm,),
        in_specs=[pl.BlockSpec((tm, d), lambda i: (i, 0)),
                  pl.BlockSpec((tm, p2d.shape[1]), lambda i: (i, 0)),
                  full(wqT), full(k1), full(k2), full(wg), full(bg), full(wp)],
        out_specs=(pl.BlockSpec((tm, d), lambda i: (i, 0)), pl.BlockSpec((d // 2, tm), lambda i: (0, i)),
                   head_spec, head_spec, flat_spec, flat_spec),
        scratch_shapes=[pltpu.VMEM((wqT.shape[0], tm), BF16),
                        pltpu.VMEM((PEER_HEADS, PEER_NKEYS, tm), F32),
                        pltpu.VMEM((PEER_HEADS, PEER_NKEYS, tm), F32),
                        pltpu.VMEM((nt, SUBLANES, tm), F32),
                        pltpu.VMEM((nt, SUBLANES, tm), F32),
                        pltpu.VMEM((4, SUBLANES, tm), F32)],
        compiler_params=_cparams(("parallel",)),
        name="peer_prep",
    )(x1, p2d, wqT, k1, k2, wg, bg, wp)


def _pack_kernel(transpose, x_ref, o_ref):
    x = x_ref[...]
    if transpose:
        x = x.T
    o_ref[...] = pltpu.bitcast(x.astype(BF16), jnp.uint32)


def _pack_bf16(w, transpose=False):
    r, c = w.shape
    tr = min(1024, r)
    if transpose:
        out_shape, out_spec = (c // 2, r), pl.BlockSpec((c // 2, tr), lambda i: (0, i))
    else:
        out_shape, out_spec = (r // 2, c), pl.BlockSpec((tr // 2, c), lambda i: (i, 0))
    return pl.pallas_call(
        functools.partial(_pack_kernel, transpose),
        out_shape=jax.ShapeDtypeStruct(out_shape, jnp.uint32),
        grid=(r // tr,),
        in_specs=[pl.BlockSpec((tr, c), lambda i: (i, 0))],
        out_specs=out_spec,
        compiler_params=_cparams(("parallel",)),
        name="pack_bf16_t" if transpose else "pack_bf16",
    )(w)


def _peer_experts_kernel(xbt_ref, base_ref, nsel_ref, cw_ref, r2_ref, e2_ref, u_ref, vtp_ref, vtl_ref,
                         lng_ref, lnb_ref, o_ref, acc_ref, m_ref):
    j = pl.program_id(1)
    nj = pl.num_programs(1)
    ec, tb = m_ref.shape[1], m_ref.shape[2]
    tile = (PEER_NKEYS, LANES)
    piece = 2
    piece_rows = piece * PEER_NKEYS
    cur = j % 2
    prev = 1 - cur

    @pl.when(j == 0)
    def _():
        acc_ref[...] = jnp.zeros(acc_ref.shape, F32)
        m_ref[prev] = jnp.zeros((ec, tb), BF16)

    xbt = pltpu.bitcast(xbt_ref[...], BF16)
    n_pieces = ec // piece_rows
    d = acc_ref.shape[0]

    def scores(p):
        u_piece = pltpu.bitcast(u_ref[p * piece_rows // 2:(p + 1) * piece_rows // 2, :], BF16)
        return jnp.dot(u_piece, xbt, preferred_element_type=F32)

    def project_prev(q):
        rows = slice(q * d // n_pieces, (q + 1) * d // n_pieces)
        packed = slice(q * d // n_pieces // 2, (q + 1) * d // n_pieces // 2)
        acc_ref[rows, :] += jnp.dot(pltpu.bitcast(vtp_ref[packed, :], BF16), m_ref[prev],
                                    preferred_element_type=F32)

    hts = {0: scores(0)}
    for p in range(n_pieces):
        if p + 1 < n_pieces:
            hts[p + 1] = scores(p + 1)
        if p > 0:
            project_prev(p - 1)
        if p == n_pieces - 1:
            project_prev(p)
        ht = hts.pop(p)
        for c in range(tb // LANES):
            cols = slice(c * LANES, (c + 1) * LANES)
            w = [None] * piece
            for hd in range(PEER_HEADS):
                rank2 = pltpu.bitcast(r2_ref[hd * PACKED_KEYS:(hd + 1) * PACKED_KEYS, cols], BF16)
                ef = pltpu.bitcast(e2_ref[hd * PACKED_KEYS:(hd + 1) * PACKED_KEYS, cols], BF16)
                for k in range(piece):
                    r = p * piece + k
                    nsel = jnp.broadcast_to(nsel_ref[hd, r:r + 1, cols], tile).astype(BF16)
                    cw = jnp.broadcast_to(cw_ref[hd, r:r + 1, cols], tile).astype(BF16)
                    term = jnp.where(rank2 < nsel, ef * cw, jnp.zeros_like(ef))
                    w[k] = term if w[k] is None else w[k] + term
            for k in range(piece):
                act = jax.nn.gelu(ht[k * PEER_NKEYS:(k + 1) * PEER_NKEYS, cols].astype(BF16))
                rows = slice((p * piece + k) * PEER_NKEYS, (p * piece + k + 1) * PEER_NKEYS)
                m_ref[cur, rows, cols] = act * w[k]

    @pl.when(j == nj - 1)
    def _():
        acc = acc_ref[...] + jnp.dot(pltpu.bitcast(vtl_ref[...], BF16), m_ref[cur],
                                     preferred_element_type=F32)
        o_ref[...] = _layer_norm(base_ref[...] + acc.T, lng_ref[...], lnb_ref[...])


def _peer_experts(xbt_packed, base, nsel, cw, r2, e2, u_packed, vt_packed, ln_g, ln_b):
    t, d = base.shape
    ne = vt_packed.shape[1]
    tb = min(1024, t)
    ec = SUBLANES * PEER_NKEYS
    nj = ne // ec
    once = pl.Buffered(1)
    full = lambda a: pl.BlockSpec(a.shape, lambda i, j: (0,) * a.ndim, pipeline_mode=once)
    head_spec = pl.BlockSpec((PEER_HEADS, SUBLANES, tb), lambda i, j: (0, j, i))
    flat_spec = pl.BlockSpec((PEER_HEADS * PACKED_KEYS, tb), lambda i, j: (0, i), pipeline_mode=once)
    return pl.pallas_call(
        _peer_experts_kernel,
        out_shape=jax.ShapeDtypeStruct((t, d), F32),
        grid=(t // tb, nj),
        in_specs=[pl.BlockSpec((d // 2, tb), lambda i, j: (0, i), pipeline_mode=once),
                  pl.BlockSpec((tb, d), lambda i, j: (i, 0), pipeline_mode=once),
                  head_spec, head_spec, flat_spec, flat_spec,
                  pl.BlockSpec((ec // 2, d), lambda i, j: (j, 0)),
                  pl.BlockSpec((d // 2, ec), lambda i, j: (0, jnp.maximum(j - 1, 0))),
                  pl.BlockSpec((d // 2, ec), lambda i, j: (0, nj - 1), pipeline_mode=once),
                  full(ln_g), full(ln_b)],
        out_specs=pl.BlockSpec((tb, d), lambda i, j: (i, 0)),
        scratch_shapes=[pltpu.VMEM((d, tb), F32), pltpu.VMEM((2, ec, tb), BF16)],
        compiler_params=_cparams(("parallel", "arbitrary")),
        name="peer_experts",
    )(xbt_packed, base, nsel, cw, r2, e2, u_packed, vt_packed, vt_packed, ln_g, ln_b)


def _pad_cols(a, n):
    return jnp.pad(a, ((0, 0), (0, n - a.shape[1])))


def _layer(x2d, p2d, bsz, alpha, w_in, b_gate, gm_ln_g, gm_ln_b, gm_w_s, gm_b_s, gm_w_out,
           ssd_conv_w, ssd_conv_b, ssd_dt_bias, ssd_a_log, ssd_d, ssd_norm_w, ssd_w_out,
           w_o, ln1_g, ln1_b, peer_w_q, peer_k1, peer_k2, peer_u, peer_v,
           ple_w_proj, ple_w_gate, ple_b_gate, ln2_g, ln2_b):
    d = x2d.shape[1]
    width = gm_ln_g.shape[0]
    inner = ssd_norm_w.shape[0]
    conv_dim = ssd_conv_b.shape[0]
    heads = ssd_a_log.shape[0]
    o1 = 2 * width
    o2 = o1 + inner
    o3 = o2 + conv_dim
    o4 = o3 + heads
    row = lambda v: v.reshape(1, -1)

    w_main = jnp.concatenate([w_in[:, :o3], w_in[:, o4:]], axis=1).astype(BF16)
    h = _inproj(x2d, w_main)
    gate_a_block = o3 // d
    gate_b_block = gate_a_block + 1

    causal = jnp.tril(jnp.ones((CHUNK, CHUNK), dtype=bool))
    wc = jnp.where(causal[None], gm_w_s, 0.0).astype(BF16)
    bs_exp = jnp.repeat(gm_b_s.T, width // GM_GROUPS, axis=1)
    ya_g = _gmlp(h, row(gm_ln_g), row(gm_ln_b), wc, bs_exp, gm_w_out.astype(BF16),
                 row(b_gate[:d]), gate_a_block)

    w_dt = _pad_cols(w_in[:, o3:o4], LANES)
    pad_row = lambda v: _pad_cols(row(v), LANES)
    hid = jnp.arange(LANES)[:, None]
    e64 = (hid == (jnp.arange(inner)[None, :] // SSD_HEAD_DIM)).astype(BF16)
    e128 = (hid == (jnp.arange(heads * LANES)[None, :] // LANES)).astype(BF16)
    ltri = causal.astype(BF16)
    consts = (w_dt.astype(BF16), w_dt.T.astype(BF16), pad_row(ssd_dt_bias), pad_row(ssd_dt_bias).T,
              pad_row(ssd_a_log), pad_row(ssd_a_log).T, ssd_conv_w, row(ssd_conv_b),
              row(jnp.repeat(ssd_d, SSD_HEAD_DIM)), row(ssd_norm_w), e64, e128, ltri, ltri.T)
    y_norm = _ssd(x2d, h, bsz, consts, o1 // inner, o2 // conv_dim, inner, conv_dim)

    x1 = _merge(alpha, ya_g, y_norm, h, row(b_gate[d:]), ssd_w_out.astype(BF16), w_o.astype(BF16),
                x2d, row(ln1_g), row(ln1_b), gate_b_block)

    base, xbt, nsel, cw, r2, e2 = _peer_prep(
        alpha, x1, p2d, peer_w_q.T.astype(BF16), peer_k1.astype(BF16), peer_k2.astype(BF16),
        ple_w_gate.astype(BF16), row(ple_b_gate), ple_w_proj.astype(BF16))
    return _peer_experts(xbt, base, nsel, cw, r2, e2, _pack_bf16(peer_u),
                         _pack_bf16(peer_v, transpose=True), row(ln2_g), row(ln2_b))


def kernel(x, p, w_in, b_gate, gm_ln_g, gm_ln_b, gm_w_s, gm_b_s, gm_w_out, ssd_conv_w, ssd_conv_b,
           ssd_dt_bias, ssd_a_log, ssd_d, ssd_norm_w, ssd_w_out, w_o, ln1_g, ln1_b, peer_w_q,
           peer_k1, peer_k2, peer_u, peer_v, ple_w_proj, ple_w_gate, ple_b_gate, ln2_g, ln2_b):
    bsz, s, d = x.shape
    depth = w_in.shape[0]
    alpha = (2 * depth) ** 0.25
    x2d = x.reshape(bsz * s, d)
    for i in range(depth):
        x2d = _layer(x2d, p[i].reshape(bsz * s, -1), bsz, alpha, w_in[i], b_gate[i], gm_ln_g[i],
                     gm_ln_b[i], gm_w_s[i], gm_b_s[i], gm_w_out[i], ssd_conv_w[i], ssd_conv_b[i],
                     ssd_dt_bias[i], ssd_a_log[i], ssd_d[i], ssd_norm_w[i], ssd_w_out[i], w_o[i],
                     ln1_g[i], ln1_b[i], peer_w_q[i], peer_k1[i], peer_k2[i], peer_u[i], peer_v[i],
                     ple_w_proj[i], ple_w_gate[i], ple_b_gate[i], ln2_g[i], ln2_b[i])
    return x2d.reshape(bsz, s, d)
```

```python
import functools
import math

import jax
import jax.numpy as jnp
from jax import lax
from jax.experimental import pallas as pl
from jax.experimental.pallas import tpu as pltpu

BF16 = jnp.bfloat16
F32 = jnp.float32

LN_EPS = 1e-5
CHUNK = 128
GM_GROUPS = 8
SSD_HEAD_DIM = 64
SSD_STATE = 128
SSD_GROUPS = 8
SSD_CONV = 4
PEER_HEADS = 8
PEER_NKEYS = 128
PEER_TOPK = 16
PACKED_KEYS = PEER_NKEYS // 2
LANES = 128
SUBLANES = 8
MXU_COLS = 256
VMEM_LIMIT = 60 * 1024 * 1024

NT_DIMS = (((1,), (1,)), ((), ()))


def _cparams(sem):
    return pltpu.CompilerParams(dimension_semantics=sem, vmem_limit_bytes=VMEM_LIMIT)


def _layer_norm(t, g, b):
    mu = jnp.mean(t, axis=-1, keepdims=True)
    d = t - mu
    var = jnp.mean(d * d, axis=-1, keepdims=True)
    return d * lax.rsqrt(var + LN_EPS) * g + b


def _split3(v):
    hi = v.astype(BF16)
    r = v - hi.astype(F32)
    mid = r.astype(BF16)
    lo = (r - mid.astype(F32)).astype(BF16)
    return hi, mid, lo


def _inproj_kernel(x_ref, w_ref, o_ref, xb_ref):
    @pl.when(pl.program_id(1) == 0)
    def _():
        xb_ref[...] = x_ref[...].astype(BF16)

    o_ref[...] = jnp.dot(xb_ref[...], w_ref[...], preferred_element_type=F32).astype(o_ref.dtype)


def _inproj(x2d, w):
    t, d = x2d.shape
    n = w.shape[1]
    tm = min(1024, t)
    tn = 2048
    return pl.pallas_call(
        _inproj_kernel,
        out_shape=jax.ShapeDtypeStruct((t, n), BF16),
        grid=(t // tm, n // tn),
        in_specs=[pl.BlockSpec((tm, d), lambda i, j: (i, 0)),
                  pl.BlockSpec((d, tn), lambda i, j: (0, j))],
        out_specs=pl.BlockSpec((tm, tn), lambda i, j: (i, j)),
        scratch_shapes=[pltpu.VMEM((tm, d), BF16)],
        compiler_params=_cparams(("parallel", "arbitrary")),
        name="inproj",
    )(x2d, w)


def _gmlp_kernel(uv_ref, gp_ref, lng_ref, lnb_ref, wc_ref, bs_ref, wout_ref, bg_ref,
                 o_ref, pre_ref):
    width = lng_ref.shape[1]
    gd = width // GM_GROUPS
    for c in range(uv_ref.shape[0] // CHUNK):
        rows = slice(c * CHUNK, (c + 1) * CHUNK)
        g = jax.nn.gelu(uv_ref[rows, :].astype(F32))
        u = g[:, :width]
        v = _layer_norm(g[:, width:], lng_ref[...], lnb_ref[...]).astype(BF16)
        for grp in range(GM_GROUPS):
            cols = slice(grp * gd, (grp + 1) * gd)
            mixed = jnp.dot(wc_ref[grp], v[:, cols], preferred_element_type=F32)
            pre_ref[rows, cols] = (u[:, cols] * (mixed + bs_ref[:, cols])).astype(BF16)
    ya = jnp.dot(pre_ref[...], wout_ref[...], preferred_element_type=F32)
    o_ref[...] = jax.nn.sigmoid(gp_ref[...] + bg_ref[...]) * ya


def _gmlp(h, ln_g, ln_b, wc, bs_exp, w_out, b_gate_a, gate_col_block):
    t = h.shape[0]
    width = ln_g.shape[1]
    tm = min(512, t)
    full = lambda shape: pl.BlockSpec(shape, lambda i: (0,) * len(shape))
    return pl.pallas_call(
        _gmlp_kernel,
        out_shape=jax.ShapeDtypeStruct((t, width), F32),
        grid=(t // tm,),
        in_specs=[pl.BlockSpec((tm, 2 * width), lambda i: (i, 0)),
                  pl.BlockSpec((tm, width), lambda i: (i, gate_col_block)),
                  full(ln_g.shape), full(ln_b.shape), full(wc.shape), full(bs_exp.shape),
                  full(w_out.shape), full(b_gate_a.shape)],
        out_specs=pl.BlockSpec((tm, width), lambda i: (i, 0)),
        scratch_shapes=[pltpu.VMEM((tm, width), BF16)],
        compiler_params=_cparams(("parallel",)),
        name="gmlp",
    )(h, h, ln_g, ln_b, wc, bs_exp, w_out, b_gate_a)


def _ssd_kernel(x_ref, z_ref, xbc_ref, wdt_ref, wdtT_ref, dtb_row_ref, dtb_col_ref,
                alog_row_ref, alog_col_ref, convw_ref, convb_ref, dexp_ref, normw_ref,
                e64_ref, e128_ref, ltri_ref, utri_ref,
                o_ref, xe_ref, st_ref, y_ref):
    inner = z_ref.shape[1]
    n_pairs = inner // LANES
    gs = SSD_GROUPS * SSD_STATE

    @pl.when(pl.program_id(1) == 0)
    def _():
        xe_ref[0:SUBLANES, :] = jnp.zeros((SUBLANES, xe_ref.shape[1]), F32)
        st_ref[...] = jnp.zeros(st_ref.shape, F32)

    xb = x_ref[...].astype(BF16)
    dt = jax.nn.softplus(jnp.dot(xb, wdt_ref[...], preferred_element_type=F32) + dtb_row_ref[...])
    dt_t = jax.nn.softplus(lax.dot_general(wdtT_ref[...], xb, NT_DIMS, preferred_element_type=F32)
                           + dtb_col_ref[...])
    da = dt * (-jnp.exp(alog_row_ref[...]))
    da_t = dt_t * (-jnp.exp(alog_col_ref[...]))
    da_parts = _split3(da)
    ltri = ltri_ref[...]
    acum = sum(jnp.dot(ltri, p, preferred_element_type=F32) for p in da_parts)
    utri = utri_ref[...]
    acum_t = sum(jnp.dot(p, utri, preferred_element_type=F32) for p in _split3(da_t))

    e64 = e64_ref[...]
    e128 = e128_ref[...]
    acum_parts = _split3(acum)
    dt_e = sum(jnp.dot(p, e64, preferred_element_type=F32) for p in _split3(dt))
    acum_e = sum(jnp.dot(p, e64, preferred_element_type=F32) for p in acum_parts)
    acum_b = sum(jnp.dot(p, e128, preferred_element_type=F32) for p in acum_parts)
    alast_e = acum_e[CHUNK - 1:CHUNK, :]

    xe_ref[SUBLANES:SUBLANES + CHUNK, :] = xbc_ref[...].astype(F32)
    conv = convb_ref[...]
    for k in range(SSD_CONV):
        off = SUBLANES - (SSD_CONV - 1) + k
        conv = conv + convw_ref[k:k + 1, :] * xe_ref[off:off + CHUNK, :]
    xe_ref[0:SUBLANES, :] = xe_ref[CHUNK:CHUNK + SUBLANES, :]
    xbc = jax.nn.silu(conv)

    xs = xbc[:, :inner]
    xdt = xs * dt_e
    xdt_b = xdt.astype(BF16)
    xw_b = (xdt * jnp.exp(alast_e - acum_e)).astype(BF16)
    ea = jnp.exp(acum_e)
    dec_state = jnp.exp(alast_e)
    dskip = dexp_ref[...]

    lane = lax.broadcasted_iota(jnp.int32, (CHUNK, LANES), 1)
    row = lax.broadcasted_iota(jnp.int32, (CHUNK, LANES), 0)
    causal = row >= lane
    first_half = lane < SSD_HEAD_DIM
    pairs_per_group = n_pairs // SSD_GROUPS

    for g in range(SSD_GROUPS):
        bm_f = xbc[:, inner + g * SSD_STATE: inner + (g + 1) * SSD_STATE]
        bm_g = bm_f.astype(BF16)
        bmt_g = bm_f.T.astype(BF16)
        cm_g = xbc[:, inner + gs + g * SSD_STATE: inner + gs + (g + 1) * SSD_STATE].astype(BF16)
        cb = lax.dot_general(cm_g, bm_g, NT_DIMS, preferred_element_type=F32)
        for j in range(pairs_per_group):
            pr = g * pairs_per_group + j
            cols = slice(pr * LANES, (pr + 1) * LANES)
            xdt_p = xdt_b[:, cols]
            y_p = None
            for k in range(2):
                hd = pr * 2 + k
                seg = acum_b[:, hd * LANES:(hd + 1) * LANES] - acum_t[hd:hd + 1, :]
                dec = jnp.exp(jnp.where(causal, seg, -jnp.inf))
                sc = (cb * dec).astype(BF16)
                keep = first_half if k == 0 else jnp.logical_not(first_half)
                rhs = jnp.where(keep, xdt_p, jnp.zeros_like(xdt_p))
                yk = jnp.dot(sc, rhs, preferred_element_type=F32)
                y_p = yk if y_p is None else y_p + yk
            st = st_ref[pr]
            y_off = jnp.dot(cm_g, st.astype(BF16), preferred_element_type=F32) * ea[:, cols]
            y_ref[:, cols] = y_p + y_off + xs[:, cols] * dskip[:, cols]
            upd = jnp.dot(bmt_g, xw_b[:, cols], preferred_element_type=F32)
            st_ref[pr] = dec_state[:, cols] * st + upd

    yf = y_ref[...] * jax.nn.silu(z_ref[...].astype(F32))
    ms = jnp.mean(yf * yf, axis=-1, keepdims=True)
    o_ref[...] = yf * lax.rsqrt(ms + LN_EPS) * normw_ref[...]


def _ssd(x2d, h, bsz, consts, z_col_block, xbc_col_block, inner, conv_dim):
    t, d = x2d.shape
    nc = t // bsz // CHUNK
    n_pairs = inner // LANES
    row_map = lambda b, c: (b * nc + c, 0)
    full = lambda a: pl.BlockSpec(a.shape, lambda b, c: (0,) * a.ndim)
    return pl.pallas_call(
        _ssd_kernel,
        out_shape=jax.ShapeDtypeStruct((t, inner), F32),
        grid=(bsz, nc),
        in_specs=[pl.BlockSpec((CHUNK, d), row_map),
                  pl.BlockSpec((CHUNK, inner), lambda b, c: (b * nc + c, z_col_block)),
                  pl.BlockSpec((CHUNK, conv_dim), lambda b, c: (b * nc + c, xbc_col_block))]
                 + [full(a) for a in consts],
        out_specs=pl.BlockSpec((CHUNK, inner), row_map),
        scratch_shapes=[pltpu.VMEM((CHUNK + SUBLANES, conv_dim), F32),
                        pltpu.VMEM((n_pairs, SSD_STATE, LANES), F32),
                        pltpu.VMEM((CHUNK, inner), F32)],
        compiler_params=_cparams(("parallel", "arbitrary")),
        name="ssd",
    )(x2d, h, h, *consts)


def _merge_kernel(alpha, ya_ref, yn_ref, gp_ref, bg_ref, wssd_ref, wo_ref, x_ref, lng_ref, lnb_ref,
                  o_ref):
    yb = jnp.dot(yn_ref[...].astype(BF16), wssd_ref[...], preferred_element_type=F32)
    merged = ya_ref[...] + jax.nn.sigmoid(gp_ref[...] + bg_ref[...]) * yb
    t = alpha * x_ref[...] + jnp.dot(merged.astype(BF16), wo_ref[...], preferred_element_type=F32)
    o_ref[...] = _layer_norm(t, lng_ref[...], lnb_ref[...])


def _merge(alpha, ya_g, y_norm, h, b_gate_b, w_ssd, w_o, x2d, ln_g, ln_b, gate_col_block):
    t, d = x2d.shape
    inner = y_norm.shape[1]
    tm = min(512, t)
    full = lambda a: pl.BlockSpec(a.shape, lambda i: (0,) * a.ndim)
    return pl.pallas_call(
        functools.partial(_merge_kernel, alpha),
        out_shape=jax.ShapeDtypeStruct((t, d), F32),
        grid=(t // tm,),
        in_specs=[pl.BlockSpec((tm, d), lambda i: (i, 0)),
                  pl.BlockSpec((tm, inner), lambda i: (i, 0)),
                  pl.BlockSpec((tm, d), lambda i: (i, gate_col_block)),
                  full(b_gate_b), full(w_ssd), full(w_o),
                  pl.BlockSpec((tm, d), lambda i: (i, 0)),
                  full(ln_g), full(ln_b)],
        out_specs=pl.BlockSpec((tm, d), lambda i: (i, 0)),
        compiler_params=_cparams(("parallel",)),
        name="merge",
    )(ya_g, y_norm, h, b_gate_b, w_ssd, w_o, x2d, ln_g, ln_b)


def _bitonic_merge_desc(xs):
    xs = list(xs)
    n = len(xs)
    j = n // 2
    while j >= 1:
        for i in range(n):
            l = i ^ j
            if l > i:
                xs[i], xs[l] = jnp.maximum(xs[i], xs[l]), jnp.minimum(xs[i], xs[l])
        j //= 2
    return xs


def _bitonic_sort_desc(xs):
    xs = list(xs)
    n = len(xs)
    k = 2
    while k <= n:
        j = k // 2
        while j >= 1:
            for i in range(n):
                l = i ^ j
                if l > i:
                    hi, lo = jnp.maximum(xs[i], xs[l]), jnp.minimum(xs[i], xs[l])
                    xs[i], xs[l] = (hi, lo) if (i & k) == 0 else (lo, hi)
            j //= 2
        k *= 2
    return xs


def _top16_and_next(s):
    nblk = s.shape[0] // SUBLANES
    blocks = [s[i * SUBLANES:(i + 1) * SUBLANES, :] for i in range(nblk)]
    top = _bitonic_sort_desc(blocks)
    for shift in (4, 2, 1):
        other = [pltpu.roll(r, shift, axis=0) for r in top]
        top = _bitonic_merge_desc([jnp.maximum(top[i], other[nblk - 1 - i]) for i in range(nblk)])
    kth = top[PEER_TOPK - 1]
    nxt = None
    for blk in blocks:
        cand = jnp.where(blk < kth, blk, -jnp.inf)
        nxt = cand if nxt is None else jnp.maximum(nxt, cand)
    for shift in (4, 2, 1):
        nxt = jnp.maximum(nxt, pltpu.roll(nxt, shift, axis=0))
    return top + [nxt]


def _peer_prep_kernel(alpha, x1_ref, p_ref, wqT_ref, k1_ref, k2_ref, wg_ref, bg_ref, wp_ref,
                      base_ref, xbt_ref, nsel_ref, cw_ref, r2_ref, e2_ref,
                      qT_ref, s1_ref, s2_ref, top1_ref, top2_ref, row_ref):
    tm = x1_ref.shape[0]
    x1 = x1_ref[...]
    xb = x1.astype(BF16)
    xbt = x1.T.astype(BF16)
    xbt_ref[...] = pltpu.bitcast(xbt, jnp.uint32)
    gate = jax.nn.sigmoid(jnp.dot(xb, wg_ref[...], preferred_element_type=F32) + bg_ref[...])
    ple = gate * jnp.dot(p_ref[...].astype(BF16), wp_ref[...], preferred_element_type=F32)
    base_ref[...] = alpha * x1 + ple

    qT_ref[...] = jnp.dot(wqT_ref[...], xbt, preferred_element_type=F32).astype(BF16)
    top1_ref[...] = jnp.zeros(top1_ref.shape, F32)
    top2_ref[...] = jnp.zeros(top2_ref.shape, F32)
    sub = lax.broadcasted_iota(jnp.int32, (SUBLANES, tm), 0)
    dkey = 2 * PEER_NKEYS

    def head_scores(hd, carry):
        q1 = qT_ref[pl.ds(pl.multiple_of(hd * dkey, dkey), PEER_NKEYS), :]
        q2 = qT_ref[pl.ds(pl.multiple_of(hd * dkey + PEER_NKEYS, PEER_NKEYS), PEER_NKEYS), :]
        s1 = jnp.dot(k1_ref[hd], q1, preferred_element_type=F32)
        s2 = jnp.dot(k2_ref[hd], q2, preferred_element_type=F32)
        s1_ref[hd] = s1
        s2_ref[hd] = s2
        mine = sub == hd
        for i, v in enumerate(_top16_and_next(s1)):
            top1_ref[i] = jnp.where(mine, v, top1_ref[i])
        for i, v in enumerate(_top16_and_next(s2)):
            top2_ref[i] = jnp.where(mine, v, top2_ref[i])
        return carry

    lax.fori_loop(0, PEER_HEADS, head_scores, 0)

    nt = PEER_TOPK + 1
    t1 = [top1_ref[i] for i in range(nt)]
    t2 = [top2_ref[i] for i in range(nt)]
    cands = [t1[a] + t2[b] for a in range(nt) for b in range(nt) if (a + 1) * (b + 1) <= nt]
    cands += [jnp.full((SUBLANES, tm), -jnp.inf, F32)] * (64 - len(cands))
    srt = _bitonic_sort_desc(cands)
    zsum = None
    for i in range(PEER_TOPK):
        e = jnp.exp(srt[i] - srt[0])
        zsum = e if zsum is None else zsum + e
    row_ref[0] = 0.5 * (srt[PEER_TOPK - 1] + srt[PEER_TOPK])
    row_ref[1] = t1[0]
    row_ref[2] = t2[0]
    row_ref[3] = 1.0 / zsum

    def head_factors(hd, carry):
        theta = row_ref[0, pl.ds(hd, 1), :]
        m1 = row_ref[1, pl.ds(hd, 1), :]
        m2 = row_ref[2, pl.ds(hd, 1), :]
        inv_z = row_ref[3, pl.ds(hd, 1), :]
        s1 = s1_ref[hd]
        s2 = s2_ref[hd]
        thr = theta - s1
        nsel = jnp.zeros_like(s1)
        rank2 = jnp.zeros_like(s2)
        for k in range(PEER_TOPK):
            v2k = top2_ref[k, pl.ds(hd, 1), :]
            nsel = nsel + jnp.where(v2k >= thr, 1.0, 0.0)
            rank2 = rank2 + jnp.where(v2k > s2, 1.0, 0.0)
        nsel_ref[hd] = nsel
        key_rows = pl.ds(pl.multiple_of(hd * PACKED_KEYS, PACKED_KEYS), PACKED_KEYS)
        r2_ref[key_rows, :] = pltpu.bitcast(rank2.astype(BF16), jnp.uint32)
        cw_ref[hd] = jnp.exp(s1 - m1) * inv_z
        e2_ref[key_rows, :] = pltpu.bitcast(jnp.exp(s2 - m2).astype(BF16), jnp.uint32)
        return carry

    lax.fori_loop(0, PEER_HEADS, head_factors, 0)


def _peer_prep(alpha, x1, p2d, wqT, k1, k2, wg, bg, wp):
    t, d = x1.shape
    tm = min(512, t)
    full = lambda a: pl.BlockSpec(a.shape, lambda i: (0,) * a.ndim)
    head_shape = jax.ShapeDtypeStruct((PEER_HEADS, PEER_NKEYS, t), F32)
    flat_bf16 = jax.ShapeDtypeStruct((PEER_HEADS * PACKED_KEYS, t), jnp.uint32)
    flat_spec = pl.BlockSpec((PEER_HEADS * PACKED_KEYS, tm), lambda i: (0, i))
    head_spec = pl.BlockSpec((PEER_HEADS, PEER_NKEYS, tm), lambda i: (0, 0, i))
    nt = PEER_TOPK + 1
    return pl.pallas_call(
        functools.partial(_peer_prep_kernel, alpha),
        out_shape=(jax.ShapeDtypeStruct((t, d), F32), jax.ShapeDtypeStruct((d // 2, t), jnp.uint32),
                   head_shape, head_shape, flat_bf16, flat_bf16),
        grid=(t // tm,),
        in_specs=[pl.BlockSpec((tm, d), lambda i: (i, 0)),
                  pl.BlockSpec((tm, p2d.shape[1]), lambda i: (i, 0)),
                  full(wqT), full(k1), full(k2), full(wg), full(bg), full(wp)],
        out_specs=(pl.BlockSpec((tm, d), lambda i: (i, 0)), pl.BlockSpec((d // 2, tm), lambda i: (0, i)),
                   head_spec, head_spec, flat_spec, flat_spec),
        scratch_shapes=[pltpu.VMEM((wqT.shape[0], tm), BF16),
                        pltpu.VMEM((PEER_HEADS, PEER_NKEYS, tm), F32),
                        pltpu.VMEM((PEER_HEADS, PEER_NKEYS, tm), F32),
                        pltpu.VMEM((nt, SUBLANES, tm), F32),
                        pltpu.VMEM((nt, SUBLANES, tm), F32),
                        pltpu.VMEM((4, SUBLANES, tm), F32)],
        compiler_params=_cparams(("parallel",)),
        name="peer_prep",
    )(x1, p2d, wqT, k1, k2, wg, bg, wp)


def _pack_kernel(transpose, x_ref, o_ref):
    x = x_ref[...]
    if transpose:
        x = x.T
    o_ref[...] = pltpu.bitcast(x.astype(BF16), jnp.uint32)


def _pack_bf16(w, transpose=False):
    r, c = w.shape
    tr = min(1024, r)
    if transpose:
        out_shape, out_spec = (c // 2, r), pl.BlockSpec((c // 2, tr), lambda i: (0, i))
    else:
        out_shape, out_spec = (r // 2, c), pl.BlockSpec((tr // 2, c), lambda i: (i, 0))
    return pl.pallas_call(
        functools.partial(_pack_kernel, transpose),
        out_shape=jax.ShapeDtypeStruct(out_shape, jnp.uint32),
        grid=(r // tr,),
        in_specs=[pl.BlockSpec((tr, c), lambda i: (i, 0))],
        out_specs=out_spec,
        compiler_params=_cparams(("parallel",)),
        name="pack_bf16_t" if transpose else "pack_bf16",
    )(w)


def _peer_experts_kernel(xbt_ref, base_ref, nsel_ref, cw_ref, r2_ref, e2_ref, u_ref, vtp_ref, vtl_ref,
                         lng_ref, lnb_ref, o_ref, acc_ref, m_ref):
    j = pl.program_id(1)
    nj = pl.num_programs(1)
    ec, tb = m_ref.shape[1], m_ref.shape[2]
    tile = (PEER_NKEYS, LANES)
    e1_rows = ec // PEER_NKEYS
    cur = j % 2
    prev = 1 - cur

    @pl.when(j == 0)
    def _():
        acc_ref[...] = jnp.zeros(acc_ref.shape, F32)
        m_ref[prev] = jnp.zeros((ec, tb), BF16)

    ht = jnp.dot(pltpu.bitcast(u_ref[...], BF16), pltpu.bitcast(xbt_ref[...], BF16),
                 preferred_element_type=F32)
    acc_ref[...] += jnp.dot(pltpu.bitcast(vtp_ref[...], BF16), m_ref[prev],
                            preferred_element_type=F32)
    for c in range(tb // LANES):
        cols = slice(c * LANES, (c + 1) * LANES)
        w = [None] * e1_rows
        for hd in range(PEER_HEADS):
            rank2 = pltpu.bitcast(r2_ref[hd * PACKED_KEYS:(hd + 1) * PACKED_KEYS, cols], BF16)
            ef = pltpu.bitcast(e2_ref[hd * PACKED_KEYS:(hd + 1) * PACKED_KEYS, cols], BF16)
            for r in range(e1_rows):
                nsel = jnp.broadcast_to(nsel_ref[hd, r:r + 1, cols], tile).astype(BF16)
                cw = jnp.broadcast_to(cw_ref[hd, r:r + 1, cols], tile).astype(BF16)
                term = jnp.where(rank2 < nsel, ef * cw, jnp.zeros_like(ef))
                w[r] = term if w[r] is None else w[r] + term
        for r in range(e1_rows):
            rows = slice(r * PEER_NKEYS, (r + 1) * PEER_NKEYS)
            m_ref[cur, rows, cols] = jax.nn.gelu(ht[rows, cols].astype(BF16)) * w[r]

    @pl.when(j == nj - 1)
    def _():
        acc = acc_ref[...] + jnp.dot(pltpu.bitcast(vtl_ref[...], BF16), m_ref[cur],
                                     preferred_element_type=F32)
        o_ref[...] = _layer_norm(base_ref[...] + acc.T, lng_ref[...], lnb_ref[...])


def _peer_experts(xbt_packed, base, nsel, cw, r2, e2, u_packed, vt_packed, ln_g, ln_b):
    t, d = base.shape
    ne = vt_packed.shape[1]
    tb = min(1024, t)
    ec = SUBLANES * PEER_NKEYS
    nj = ne // ec
    once = pl.Buffered(1)
    full = lambda a: pl.BlockSpec(a.shape, lambda i, j: (0,) * a.ndim, pipeline_mode=once)
    head_spec = pl.BlockSpec((PEER_HEADS, SUBLANES, tb), lambda i, j: (0, j, i))
    flat_spec = pl.BlockSpec((PEER_HEADS * PACKED_KEYS, tb), lambda i, j: (0, i), pipeline_mode=once)
    return pl.pallas_call(
        _peer_experts_kernel,
        out_shape=jax.ShapeDtypeStruct((t, d), F32),
        grid=(t // tb, nj),
        in_specs=[pl.BlockSpec((d // 2, tb), lambda i, j: (0, i), pipeline_mode=once),
                  pl.BlockSpec((tb, d), lambda i, j: (i, 0), pipeline_mode=once),
                  head_spec, head_spec, flat_spec, flat_spec,
                  pl.BlockSpec((ec // 2, d), lambda i, j: (j, 0)),
                  pl.BlockSpec((d // 2, ec), lambda i, j: (0, jnp.maximum(j - 1, 0))),
                  pl.BlockSpec((d // 2, ec), lambda i, j: (0, nj - 1), pipeline_mode=once),
                  full(ln_g), full(ln_b)],
        out_specs=pl.BlockSpec((tb, d), lambda i, j: (i, 0)),
        scratch_shapes=[pltpu.VMEM((d, tb), F32), pltpu.VMEM((2, ec, tb), BF16)],
        compiler_params=_cparams(("parallel", "arbitrary")),
        name="peer_experts",
    )(xbt_packed, base, nsel, cw, r2, e2, u_packed, vt_packed, vt_packed, ln_g, ln_b)


def _pad_cols(a, n):
    return jnp.pad(a, ((0, 0), (0, n - a.shape[1])))


def _layer(x2d, p2d, bsz, alpha, w_in, b_gate, gm_ln_g, gm_ln_b, gm_w_s, gm_b_s, gm_w_out,
           ssd_conv_w, ssd_conv_b, ssd_dt_bias, ssd_a_log, ssd_d, ssd_norm_w, ssd_w_out,
           w_o, ln1_g, ln1_b, peer_w_q, peer_k1, peer_k2, peer_u, peer_v,
           ple_w_proj, ple_w_gate, ple_b_gate, ln2_g, ln2_b):
    d = x2d.shape[1]
    width = gm_ln_g.shape[0]
    inner = ssd_norm_w.shape[0]
    conv_dim = ssd_conv_b.shape[0]
    heads = ssd_a_log.shape[0]
    o1 = 2 * width
    o2 = o1 + inner
    o3 = o2 + conv_dim
    o4 = o3 + heads
    row = lambda v: v.reshape(1, -1)

    w_main = jnp.concatenate([w_in[:, :o3], w_in[:, o4:]], axis=1).astype(BF16)
    h = _inproj(x2d, w_main)
    gate_a_block = o3 // d
    gate_b_block = gate_a_block + 1

    causal = jnp.tril(jnp.ones((CHUNK, CHUNK), dtype=bool))
    wc = jnp.where(causal[None], gm_w_s, 0.0).astype(BF16)
    bs_exp = jnp.repeat(gm_b_s.T, width // GM_GROUPS, axis=1)
    ya_g = _gmlp(h, row(gm_ln_g), row(gm_ln_b), wc, bs_exp, gm_w_out.astype(BF16),
                 row(b_gate[:d]), gate_a_block)

    w_dt = _pad_cols(w_in[:, o3:o4], LANES)
    pad_row = lambda v: _pad_cols(row(v), LANES)
    hid = jnp.arange(LANES)[:, None]
    e64 = (hid == (jnp.arange(inner)[None, :] // SSD_HEAD_DIM)).astype(BF16)
    e128 = (hid == (jnp.arange(heads * LANES)[None, :] // LANES)).astype(BF16)
    ltri = causal.astype(BF16)
    consts = (w_dt.astype(BF16), w_dt.T.astype(BF16), pad_row(ssd_dt_bias), pad_row(ssd_dt_bias).T,
              pad_row(ssd_a_log), pad_row(ssd_a_log).T, ssd_conv_w, row(ssd_conv_b),
              row(jnp.repeat(ssd_d, SSD_HEAD_DIM)), row(ssd_norm_w), e64, e128, ltri, ltri.T)
    y_norm = _ssd(x2d, h, bsz, consts, o1 // inner, o2 // conv_dim, inner, conv_dim)

    x1 = _merge(alpha, ya_g, y_norm, h, row(b_gate[d:]), ssd_w_out.astype(BF16), w_o.astype(BF16),
                x2d, row(ln1_g), row(ln1_b), gate_b_block)

    base, xbt, nsel, cw, r2, e2 = _peer_prep(
        alpha, x1, p2d, peer_w_q.T.astype(BF16), peer_k1.astype(BF16), peer_k2.astype(BF16),
        ple_w_gate.astype(BF16), row(ple_b_gate), ple_w_proj.astype(BF16))
    return _peer_experts(xbt, base, nsel, cw, r2, e2, _pack_bf16(peer_u),
                         _pack_bf16(peer_v, transpose=True), row(ln2_g), row(ln2_b))


def kernel(x, p, w_in, b_gate, gm_ln_g, gm_ln_b, gm_w_s, gm_b_s, gm_w_out, ssd_conv_w, ssd_conv_b,
           ssd_dt_bias, ssd_a_log, ssd_d, ssd_norm_w, ssd_w_out, w_o, ln1_g, ln1_b, peer_w_q,
           peer_k1, peer_k2, peer_u, peer_v, ple_w_proj, ple_w_gate, ple_b_gate, ln2_g, ln2_b):
    bsz, s, d = x.shape
    depth = w_in.shape[0]
    alpha = (2 * depth) ** 0.25
    x2d = x.reshape(bsz * s, d)
    for i in range(depth):
        x2d = _layer(x2d, p[i].reshape(bsz * s, -1), bsz, alpha, w_in[i], b_gate[i], gm_ln_g[i],
                     gm_ln_b[i], gm_w_s[i], gm_b_s[i], gm_w_out[i], ssd_conv_w[i], ssd_conv_b[i],
                     ssd_dt_bias[i], ssd_a_log[i], ssd_d[i], ssd_norm_w[i], ssd_w_out[i], w_o[i],
                     ln1_g[i], ln1_b[i], peer_w_q[i], peer_k1[i], peer_k2[i], peer_u[i], peer_v[i],
                     ple_w_proj[i], ple_w_gate[i], ple_b_gate[i], ln2_g[i], ln2_b[i])
    return x2d.reshape(bsz, s, d)
```

```python
import functools
import math

import jax
import jax.numpy as jnp
from jax import lax
from jax.experimental import pallas as pl
from jax.experimental.pallas import tpu as pltpu

BF16 = jnp.bfloat16
F32 = jnp.float32

LN_EPS = 1e-5
CHUNK = 128
GM_GROUPS = 8
SSD_HEAD_DIM = 64
SSD_STATE = 128
SSD_GROUPS = 8
SSD_CONV = 4
PEER_HEADS = 8
PEER_NKEYS = 128
PEER_TOPK = 16
PACKED_KEYS = PEER_NKEYS // 2
LANES = 128
SUBLANES = 8
MXU_COLS = 256
VMEM_LIMIT = 60 * 1024 * 1024

NT_DIMS = (((1,), (1,)), ((), ()))


def _cparams(sem):
    return pltpu.CompilerParams(dimension_semantics=sem, vmem_limit_bytes=VMEM_LIMIT)


def _layer_norm(t, g, b):
    mu = jnp.mean(t, axis=-1, keepdims=True)
    d = t - mu
    var = jnp.mean(d * d, axis=-1, keepdims=True)
    return d * lax.rsqrt(var + LN_EPS) * g + b


def _split3(v):
    hi = v.astype(BF16)
    r = v - hi.astype(F32)
    mid = r.astype(BF16)
    lo = (r - mid.astype(F32)).astype(BF16)
    return hi, mid, lo


def _inproj_kernel(x_ref, w_ref, o_ref, xb_ref):
    @pl.when(pl.program_id(1) == 0)
    def _():
        xb_ref[...] = x_ref[...].astype(BF16)

    o_ref[...] = jnp.dot(xb_ref[...], w_ref[...], preferred_element_type=F32).astype(o_ref.dtype)


def _inproj(x2d, w):
    t, d = x2d.shape
    n = w.shape[1]
    tm = min(1024, t)
    tn = 2048
    return pl.pallas_call(
        _inproj_kernel,
        out_shape=jax.ShapeDtypeStruct((t, n), BF16),
        grid=(t // tm, n // tn),
        in_specs=[pl.BlockSpec((tm, d), lambda i, j: (i, 0)),
                  pl.BlockSpec((d, tn), lambda i, j: (0, j))],
        out_specs=pl.BlockSpec((tm, tn), lambda i, j: (i, j)),
        scratch_shapes=[pltpu.VMEM((tm, d), BF16)],
        compiler_params=_cparams(("parallel", "arbitrary")),
        name="inproj",
    )(x2d, w)


def _gmlp_kernel(uv_ref, gp_ref, lng_ref, lnb_ref, wc_ref, bs_ref, wout_ref, bg_ref,
                 o_ref, pre_ref):
    width = lng_ref.shape[1]
    gd = width // GM_GROUPS
    for c in range(uv_ref.shape[0] // CHUNK):
        rows = slice(c * CHUNK, (c + 1) * CHUNK)
        g = jax.nn.gelu(uv_ref[rows, :].astype(F32))
        u = g[:, :width]
        v = _layer_norm(g[:, width:], lng_ref[...], lnb_ref[...]).astype(BF16)
        for grp in range(GM_GROUPS):
            cols = slice(grp * gd, (grp + 1) * gd)
            mixed = jnp.dot(wc_ref[grp], v[:, cols], preferred_element_type=F32)
            pre_ref[rows, cols] = (u[:, cols] * (mixed + bs_ref[:, cols])).astype(BF16)
    ya = jnp.dot(pre_ref[...], wout_ref[...], preferred_element_type=F32)
    o_ref[...] = jax.nn.sigmoid(gp_ref[...] + bg_ref[...]) * ya


def _gmlp(h, ln_g, ln_b, wc, bs_exp, w_out, b_gate_a, gate_col_block):
    t = h.shape[0]
    width = ln_g.shape[1]
    tm = min(512, t)
    full = lambda shape: pl.BlockSpec(shape, lambda i: (0,) * len(shape))
    return pl.pallas_call(
        _gmlp_kernel,
        out_shape=jax.ShapeDtypeStruct((t, width), F32),
        grid=(t // tm,),
        in_specs=[pl.BlockSpec((tm, 2 * width), lambda i: (i, 0)),
                  pl.BlockSpec((tm, width), lambda i: (i, gate_col_block)),
                  full(ln_g.shape), full(ln_b.shape), full(wc.shape), full(bs_exp.shape),
                  full(w_out.shape), full(b_gate_a.shape)],
        out_specs=pl.BlockSpec((tm, width), lambda i: (i, 0)),
        scratch_shapes=[pltpu.VMEM((tm, width), BF16)],
        compiler_params=_cparams(("parallel",)),
        name="gmlp",
    )(h, h, ln_g, ln_b, wc, bs_exp, w_out, b_gate_a)


def _ssd_kernel(x_ref, z_ref, xbc_ref, wdt_ref, wdtT_ref, dtb_row_ref, dtb_col_ref,
                alog_row_ref, alog_col_ref, convw_ref, convb_ref, dexp_ref, normw_ref,
                e64_ref, e128_ref, ltri_ref, utri_ref,
                o_ref, xe_ref, st_ref, y_ref):
    inner = z_ref.shape[1]
    n_pairs = inner // LANES
    gs = SSD_GROUPS * SSD_STATE

    @pl.when(pl.program_id(1) == 0)
    def _():
        xe_ref[0:SUBLANES, :] = jnp.zeros((SUBLANES, xe_ref.shape[1]), F32)
        st_ref[...] = jnp.zeros(st_ref.shape, F32)

    xb = x_ref[...].astype(BF16)
    dt = jax.nn.softplus(jnp.dot(xb, wdt_ref[...], preferred_element_type=F32) + dtb_row_ref[...])
    dt_t = jax.nn.softplus(lax.dot_general(wdtT_ref[...], xb, NT_DIMS, preferred_element_type=F32)
                           + dtb_col_ref[...])
    da = dt * (-jnp.exp(alog_row_ref[...]))
    da_t = dt_t * (-jnp.exp(alog_col_ref[...]))
    da_parts = _split3(da)
    ltri = ltri_ref[...]
    acum = sum(jnp.dot(ltri, p, preferred_element_type=F32) for p in da_parts)
    utri = utri_ref[...]
    acum_t = sum(jnp.dot(p, utri, preferred_element_type=F32) for p in _split3(da_t))

    e64 = e64_ref[...]
    e128 = e128_ref[...]
    acum_cat = jnp.concatenate(_split3(acum), axis=1)
    dt_cat = jnp.concatenate(_split3(dt), axis=1)
    dt_e = jnp.dot(dt_cat, e64, preferred_element_type=F32)
    acum_e = jnp.dot(acum_cat, e64, preferred_element_type=F32)
    acum_b = jnp.dot(acum_cat, e128, preferred_element_type=F32)
    alast_e = acum_e[CHUNK - 1:CHUNK, :]

    x_cur = xbc_ref[...].astype(F32)
    xe_ref[SUBLANES:SUBLANES + CHUNK, :] = x_cur
    xe = xe_ref[...]
    conv = convb_ref[...] + convw_ref[SSD_CONV - 1:SSD_CONV, :] * x_cur
    for k in range(SSD_CONV - 1):
        shifted = pltpu.roll(xe, SSD_CONV - 1 - k, axis=0)[SUBLANES:SUBLANES + CHUNK, :]
        conv = conv + convw_ref[k:k + 1, :] * shifted
    xe_ref[0:SUBLANES, :] = xe_ref[CHUNK:CHUNK + SUBLANES, :]
    xbc = jax.nn.silu(conv)

    xs = xbc[:, :inner]
    xdt = xs * dt_e
    xdt_b = xdt.astype(BF16)
    xw_b = (xdt * jnp.exp(alast_e - acum_e)).astype(BF16)
    ea = jnp.exp(acum_e)
    dec_state = jnp.exp(alast_e)
    dskip = dexp_ref[...]

    lane = lax.broadcasted_iota(jnp.int32, (CHUNK, LANES), 1)
    row = lax.broadcasted_iota(jnp.int32, (CHUNK, LANES), 0)
    causal = row >= lane
    first_half = lane < SSD_HEAD_DIM
    pairs_per_group = n_pairs // SSD_GROUPS

    for g in range(SSD_GROUPS):
        bm_f = xbc[:, inner + g * SSD_STATE: inner + (g + 1) * SSD_STATE]
        bm_g = bm_f.astype(BF16)
        bmt_g = bm_f.T.astype(BF16)
        cm_g = xbc[:, inner + gs + g * SSD_STATE: inner + gs + (g + 1) * SSD_STATE].astype(BF16)
        cb = lax.dot_general(cm_g, bm_g, NT_DIMS, preferred_element_type=F32)
        for j in range(pairs_per_group):
            pr = g * pairs_per_group + j
            cols = slice(pr * LANES, (pr + 1) * LANES)
            xdt_p = xdt_b[:, cols]
            y_p = None
            for k in range(2):
                hd = pr * 2 + k
                seg = acum_b[:, hd * LANES:(hd + 1) * LANES] - acum_t[hd:hd + 1, :]
                dec = jnp.exp(jnp.where(causal, seg, -jnp.inf))
                sc = (cb * dec).astype(BF16)
                keep = first_half if k == 0 else jnp.logical_not(first_half)
                rhs = jnp.where(keep, xdt_p, jnp.zeros_like(xdt_p))
                yk = jnp.dot(sc, rhs, preferred_element_type=F32)
                y_p = yk if y_p is None else y_p + yk
            st = st_ref[pr]
            y_off = jnp.dot(cm_g, st.astype(BF16), preferred_element_type=F32) * ea[:, cols]
            y_ref[:, cols] = y_p + y_off + xs[:, cols] * dskip[:, cols]
            upd = jnp.dot(bmt_g, xw_b[:, cols], preferred_element_type=F32)
            st_ref[pr] = dec_state[:, cols] * st + upd

    yf = y_ref[...] * jax.nn.silu(z_ref[...].astype(F32))
    ms = jnp.mean(yf * yf, axis=-1, keepdims=True)
    o_ref[...] = yf * lax.rsqrt(ms + LN_EPS) * normw_ref[...]


def _ssd(x2d, h, bsz, consts, z_col_block, xbc_col_block, inner, conv_dim):
    t, d = x2d.shape
    nc = t // bsz // CHUNK
    n_pairs = inner // LANES
    row_map = lambda b, c: (b * nc + c, 0)
    full = lambda a: pl.BlockSpec(a.shape, lambda b, c: (0,) * a.ndim)
    return pl.pallas_call(
        _ssd_kernel,
        out_shape=jax.ShapeDtypeStruct((t, inner), F32),
        grid=(bsz, nc),
        in_specs=[pl.BlockSpec((CHUNK, d), row_map),
                  pl.BlockSpec((CHUNK, inner), lambda b, c: (b * nc + c, z_col_block)),
                  pl.BlockSpec((CHUNK, conv_dim), lambda b, c: (b * nc + c, xbc_col_block))]
                 + [full(a) for a in consts],
        out_specs=pl.BlockSpec((CHUNK, inner), row_map),
        scratch_shapes=[pltpu.VMEM((CHUNK + SUBLANES, conv_dim), F32),
                        pltpu.VMEM((n_pairs, SSD_STATE, LANES), F32),
                        pltpu.VMEM((CHUNK, inner), F32)],
        compiler_params=_cparams(("parallel", "arbitrary")),
        name="ssd",
    )(x2d, h, h, *consts)


def _merge_kernel(alpha, ya_ref, yn_ref, gp_ref, bg_ref, wssd_ref, wo_ref, x_ref, lng_ref, lnb_ref,
                  o_ref):
    yb = jnp.dot(yn_ref[...].astype(BF16), wssd_ref[...], preferred_element_type=F32)
    merged = ya_ref[...] + jax.nn.sigmoid(gp_ref[...] + bg_ref[...]) * yb
    t = alpha * x_ref[...] + jnp.dot(merged.astype(BF16), wo_ref[...], preferred_element_type=F32)
    o_ref[...] = _layer_norm(t, lng_ref[...], lnb_ref[...])


def _merge(alpha, ya_g, y_norm, h, b_gate_b, w_ssd, w_o, x2d, ln_g, ln_b, gate_col_block):
    t, d = x2d.shape
    inner = y_norm.shape[1]
    tm = min(512, t)
    full = lambda a: pl.BlockSpec(a.shape, lambda i: (0,) * a.ndim)
    return pl.pallas_call(
        functools.partial(_merge_kernel, alpha),
        out_shape=jax.ShapeDtypeStruct((t, d), F32),
        grid=(t // tm,),
        in_specs=[pl.BlockSpec((tm, d), lambda i: (i, 0)),
                  pl.BlockSpec((tm, inner), lambda i: (i, 0)),
                  pl.BlockSpec((tm, d), lambda i: (i, gate_col_block)),
                  full(b_gate_b), full(w_ssd), full(w_o),
                  pl.BlockSpec((tm, d), lambda i: (i, 0)),
                  full(ln_g), full(ln_b)],
        out_specs=pl.BlockSpec((tm, d), lambda i: (i, 0)),
        compiler_params=_cparams(("parallel",)),
        name="merge",
    )(ya_g, y_norm, h, b_gate_b, w_ssd, w_o, x2d, ln_g, ln_b)


def _bitonic_merge_desc(xs):
    xs = list(xs)
    n = len(xs)
    j = n // 2
    while j >= 1:
        for i in range(n):
            l = i ^ j
            if l > i:
                xs[i], xs[l] = jnp.maximum(xs[i], xs[l]), jnp.minimum(xs[i], xs[l])
        j //= 2
    return xs


def _bitonic_sort_desc(xs):
    xs = list(xs)
    n = len(xs)
    k = 2
    while k <= n:
        j = k // 2
        while j >= 1:
            for i in range(n):
                l = i ^ j
                if l > i:
                    hi, lo = jnp.maximum(xs[i], xs[l]), jnp.minimum(xs[i], xs[l])
                    xs[i], xs[l] = (hi, lo) if (i & k) == 0 else (lo, hi)
            j //= 2
        k *= 2
    return xs


def _top16_and_next(s):
    nblk = s.shape[0] // SUBLANES
    blocks = [s[i * SUBLANES:(i + 1) * SUBLANES, :] for i in range(nblk)]
    top = _bitonic_sort_desc(blocks)
    for shift in (4, 2, 1):
        other = [pltpu.roll(r, shift, axis=0) for r in top]
        top = _bitonic_merge_desc([jnp.maximum(top[i], other[nblk - 1 - i]) for i in range(nblk)])
    kth = top[PEER_TOPK - 1]
    nxt = None
    for blk in blocks:
        cand = jnp.where(blk < kth, blk, -jnp.inf)
        nxt = cand if nxt is None else jnp.maximum(nxt, cand)
    for shift in (4, 2, 1):
        nxt = jnp.maximum(nxt, pltpu.roll(nxt, shift, axis=0))
    return top + [nxt]


def _peer_prep_kernel(alpha, x1_ref, p_ref, wqT_ref, k1_ref, k2_ref, wg_ref, bg_ref, wp_ref,
                      base_ref, xbt_ref, nsel_ref, cw_ref, r2_ref, e2_ref,
                      qT_ref, s1_ref, s2_ref, top1_ref, top2_ref, row_ref):
    tm = x1_ref.shape[0]
    x1 = x1_ref[...]
    xb = x1.astype(BF16)
    xbt = x1.T.astype(BF16)
    xbt_ref[...] = pltpu.bitcast(xbt, jnp.uint32)
    gate = jax.nn.sigmoid(jnp.dot(xb, wg_ref[...], preferred_element_type=F32) + bg_ref[...])
    ple = gate * jnp.dot(p_ref[...].astype(BF16), wp_ref[...], preferred_element_type=F32)
    base_ref[...] = alpha * x1 + ple

    qT_ref[...] = jnp.dot(wqT_ref[...], xbt, preferred_element_type=F32).astype(BF16)
    top1_ref[...] = jnp.zeros(top1_ref.shape, F32)
    top2_ref[...] = jnp.zeros(top2_ref.shape, F32)
    sub = lax.broadcasted_iota(jnp.int32, (SUBLANES, tm), 0)
    dkey = 2 * PEER_NKEYS

    def head_scores(hd, carry):
        q1 = qT_ref[pl.ds(pl.multiple_of(hd * dkey, dkey), PEER_NKEYS), :]
        q2 = qT_ref[pl.ds(pl.multiple_of(hd * dkey + PEER_NKEYS, PEER_NKEYS), PEER_NKEYS), :]
        s1 = jnp.dot(k1_ref[hd], q1, preferred_element_type=F32)
        s2 = jnp.dot(k2_ref[hd], q2, preferred_element_type=F32)
        s1_ref[hd] = s1
        s2_ref[hd] = s2
        mine = sub == hd
        for i, v in enumerate(_top16_and_next(s1)):
            top1_ref[i] = jnp.where(mine, v, top1_ref[i])
        for i, v in enumerate(_top16_and_next(s2)):
            top2_ref[i] = jnp.where(mine, v, top2_ref[i])
        return carry

    lax.fori_loop(0, PEER_HEADS, head_scores, 0)

    nt = PEER_TOPK + 1
    t1 = [top1_ref[i] for i in range(nt)]
    t2 = [top2_ref[i] for i in range(nt)]
    cands = [t1[a] + t2[b] for a in range(nt) for b in range(nt) if (a + 1) * (b + 1) <= nt]
    cands += [jnp.full((SUBLANES, tm), -jnp.inf, F32)] * (64 - len(cands))
    srt = _bitonic_sort_desc(cands)
    zsum = None
    for i in range(PEER_TOPK):
        e = jnp.exp(srt[i] - srt[0])
        zsum = e if zsum is None else zsum + e
    row_ref[0] = 0.5 * (srt[PEER_TOPK - 1] + srt[PEER_TOPK])
    row_ref[1] = t1[0]
    row_ref[2] = t2[0]
    row_ref[3] = 1.0 / zsum

    def head_factors(hd, carry):
        theta = row_ref[0, pl.ds(hd, 1), :]
        m1 = row_ref[1, pl.ds(hd, 1), :]
        m2 = row_ref[2, pl.ds(hd, 1), :]
        inv_z = row_ref[3, pl.ds(hd, 1), :]
        s1 = s1_ref[hd]
        s2 = s2_ref[hd]
        thr = theta - s1
        nsel = jnp.zeros_like(s1)
        rank2 = jnp.zeros_like(s2)
        for k in range(PEER_TOPK):
            v2k = top2_ref[k, pl.ds(hd, 1), :]
            nsel = nsel + jnp.where(v2k >= thr, 1.0, 0.0)
            rank2 = rank2 + jnp.where(v2k > s2, 1.0, 0.0)
        nsel_ref[hd] = nsel
        key_rows = pl.ds(pl.multiple_of(hd * PACKED_KEYS, PACKED_KEYS), PACKED_KEYS)
        r2_ref[key_rows, :] = pltpu.bitcast(rank2.astype(BF16), jnp.uint32)
        cw_ref[hd] = jnp.exp(s1 - m1) * inv_z
        e2_ref[key_rows, :] = pltpu.bitcast(jnp.exp(s2 - m2).astype(BF16), jnp.uint32)
        return carry

    lax.fori_loop(0, PEER_HEADS, head_factors, 0)


def _peer_prep(alpha, x1, p2d, wqT, k1, k2, wg, bg, wp):
    t, d = x1.shape
    tm = min(512, t)
    full = lambda a: pl.BlockSpec(a.shape, lambda i: (0,) * a.ndim)
    head_shape = jax.ShapeDtypeStruct((PEER_HEADS, PEER_NKEYS, t), F32)
    flat_bf16 = jax.ShapeDtypeStruct((PEER_HEADS * PACKED_KEYS, t), jnp.uint32)
    flat_spec = pl.BlockSpec((PEER_HEADS * PACKED_KEYS, tm), lambda i: (0, i))
    head_spec = pl.BlockSpec((PEER_HEADS, PEER_NKEYS, tm), lambda i: (0, 0, i))
    nt = PEER_TOPK + 1
    return pl.pallas_call(
        functools.partial(_peer_prep_kernel, alpha),
        out_shape=(jax.ShapeDtypeStruct((t, d), F32), jax.ShapeDtypeStruct((d // 2, t), jnp.uint32),
                   head_shape, head_shape, flat_bf16, flat_bf16),
        grid=(t // tm,),
        in_specs=[pl.BlockSpec((tm, d), lambda i: (i, 0)),
                  pl.BlockSpec((tm, p2d.shape[1]), lambda i: (i, 0)),
                  full(wqT), full(k1), full(k2), full(wg), full(bg), full(wp)],
        out_specs=(pl.BlockSpec((tm, d), lambda i: (i, 0)), pl.BlockSpec((d // 2, tm), lambda i: (0, i)),
                   head_spec, head_spec, flat_spec, flat_spec),
        scratch_shapes=[pltpu.VMEM((wqT.shape[0], tm), BF16),
                        pltpu.VMEM((PEER_HEADS, PEER_NKEYS, tm), F32),
                        pltpu.VMEM((PEER_HEADS, PEER_NKEYS, tm), F32),
                        pltpu.VMEM((nt, SUBLANES, tm), F32),
                        pltpu.VMEM((nt, SUBLANES, tm), F32),
                        pltpu.VMEM((4, SUBLANES, tm), F32)],
        compiler_params=_cparams(("parallel",)),
        name="peer_prep",
    )(x1, p2d, wqT, k1, k2, wg, bg, wp)


def _pack_kernel(transpose, x_ref, o_ref):
    x = x_ref[...]
    if transpose:
        x = x.T
    o_ref[...] = pltpu.bitcast(x.astype(BF16), jnp.uint32)


def _pack_bf16(w, transpose=False):
    r, c = w.shape
    tr = min(1024, r)
    if transpose:
        out_shape, out_spec = (c // 2, r), pl.BlockSpec((c // 2, tr), lambda i: (0, i))
    else:
        out_shape, out_spec = (r // 2, c), pl.BlockSpec((tr // 2, c), lambda i: (i, 0))
    return pl.pallas_call(
        functools.partial(_pack_kernel, transpose),
        out_shape=jax.ShapeDtypeStruct(out_shape, jnp.uint32),
        grid=(r // tr,),
        in_specs=[pl.BlockSpec((tr, c), lambda i: (i, 0))],
        out_specs=out_spec,
        compiler_params=_cparams(("parallel",)),
        name="pack_bf16_t" if transpose else "pack_bf16",
    )(w)


def _peer_experts_kernel(xbt_ref, base_ref, nsel_ref, cw_ref, r2_ref, e2_ref, u_ref, vtp_ref, vtl_ref,
                         lng_ref, lnb_ref, o_ref, acc_ref, m_ref):
    j = pl.program_id(1)
    nj = pl.num_programs(1)
    ec, tb = m_ref.shape[1], m_ref.shape[2]
    tile = (PEER_NKEYS, LANES)
    e1_rows = ec // PEER_NKEYS
    cur = j % 2
    prev = 1 - cur

    @pl.when(j == 0)
    def _():
        acc_ref[...] = jnp.zeros(acc_ref.shape, F32)
        m_ref[prev] = jnp.zeros((ec, tb), BF16)

    ht = jnp.dot(pltpu.bitcast(u_ref[...], BF16), pltpu.bitcast(xbt_ref[...], BF16),
                 preferred_element_type=F32)
    acc_ref[...] += jnp.dot(pltpu.bitcast(vtp_ref[...], BF16), m_ref[prev],
                            preferred_element_type=F32)
    for c in range(tb // LANES):
        cols = slice(c * LANES, (c + 1) * LANES)
        w = [None] * e1_rows
        for hd in range(PEER_HEADS):
            rank2 = pltpu.bitcast(r2_ref[hd * PACKED_KEYS:(hd + 1) * PACKED_KEYS, cols], BF16)
            ef = pltpu.bitcast(e2_ref[hd * PACKED_KEYS:(hd + 1) * PACKED_KEYS, cols], BF16)
            for r in range(e1_rows):
                nsel = jnp.broadcast_to(nsel_ref[hd, r:r + 1, cols], tile).astype(BF16)
                cw = jnp.broadcast_to(cw_ref[hd, r:r + 1, cols], tile).astype(BF16)
                term = jnp.where(rank2 < nsel, ef * cw, jnp.zeros_like(ef))
                w[r] = term if w[r] is None else w[r] + term
        for r in range(e1_rows):
            rows = slice(r * PEER_NKEYS, (r + 1) * PEER_NKEYS)
            m_ref[cur, rows, cols] = jax.nn.gelu(ht[rows, cols].astype(BF16)) * w[r]

    @pl.when(j == nj - 1)
    def _():
        acc = acc_ref[...] + jnp.dot(pltpu.bitcast(vtl_ref[...], BF16), m_ref[cur],
                                     preferred_element_type=F32)
        o_ref[...] = _layer_norm(base_ref[...] + acc.T, lng_ref[...], lnb_ref[...])


def _peer_experts(xbt_packed, base, nsel, cw, r2, e2, u_packed, vt_packed, ln_g, ln_b):
    t, d = base.shape
    ne = vt_packed.shape[1]
    tb = min(1024, t)
    ec = SUBLANES * PEER_NKEYS
    nj = ne // ec
    once = pl.Buffered(1)
    full = lambda a: pl.BlockSpec(a.shape, lambda i, j: (0,) * a.ndim, pipeline_mode=once)
    head_spec = pl.BlockSpec((PEER_HEADS, SUBLANES, tb), lambda i, j: (0, j, i))
    flat_spec = pl.BlockSpec((PEER_HEADS * PACKED_KEYS, tb), lambda i, j: (0, i), pipeline_mode=once)
    return pl.pallas_call(
        _peer_experts_kernel,
        out_shape=jax.ShapeDtypeStruct((t, d), F32),
        grid=(t // tb, nj),
        in_specs=[pl.BlockSpec((d // 2, tb), lambda i, j: (0, i), pipeline_mode=once),
                  pl.BlockSpec((tb, d), lambda i, j: (i, 0), pipeline_mode=once),
                  head_spec, head_spec, flat_spec, flat_spec,
                  pl.BlockSpec((ec // 2, d), lambda i, j: (j, 0)),
                  pl.BlockSpec((d // 2, ec), lambda i, j: (0, jnp.maximum(j - 1, 0))),
                  pl.BlockSpec((d // 2, ec), lambda i, j: (0, nj - 1), pipeline_mode=once),
                  full(ln_g), full(ln_b)],
        out_specs=pl.BlockSpec((tb, d), lambda i, j: (i, 0)),
        scratch_shapes=[pltpu.VMEM((d, tb), F32), pltpu.VMEM((2, ec, tb), BF16)],
        compiler_params=_cparams(("parallel", "arbitrary")),
        name="peer_experts",
    )(xbt_packed, base, nsel, cw, r2, e2, u_packed, vt_packed, vt_packed, ln_g, ln_b)


def _pad_cols(a, n):
    return jnp.pad(a, ((0, 0), (0, n - a.shape[1])))


def _layer(x2d, p2d, bsz, alpha, w_in, b_gate, gm_ln_g, gm_ln_b, gm_w_s, gm_b_s, gm_w_out,
           ssd_conv_w, ssd_conv_b, ssd_dt_bias, ssd_a_log, ssd_d, ssd_norm_w, ssd_w_out,
           w_o, ln1_g, ln1_b, peer_w_q, peer_k1, peer_k2, peer_u, peer_v,
           ple_w_proj, ple_w_gate, ple_b_gate, ln2_g, ln2_b):
    d = x2d.shape[1]
    width = gm_ln_g.shape[0]
    inner = ssd_norm_w.shape[0]
    conv_dim = ssd_conv_b.shape[0]
    heads = ssd_a_log.shape[0]
    o1 = 2 * width
    o2 = o1 + inner
    o3 = o2 + conv_dim
    o4 = o3 + heads
    row = lambda v: v.reshape(1, -1)

    w_main = jnp.concatenate([w_in[:, :o3], w_in[:, o4:]], axis=1).astype(BF16)
    h = _inproj(x2d, w_main)
    gate_a_block = o3 // d
    gate_b_block = gate_a_block + 1

    causal = jnp.tril(jnp.ones((CHUNK, CHUNK), dtype=bool))
    wc = jnp.where(causal[None], gm_w_s, 0.0).astype(BF16)
    bs_exp = jnp.repeat(gm_b_s.T, width // GM_GROUPS, axis=1)
    ya_g = _gmlp(h, row(gm_ln_g), row(gm_ln_b), wc, bs_exp, gm_w_out.astype(BF16),
                 row(b_gate[:d]), gate_a_block)

    w_dt = _pad_cols(w_in[:, o3:o4], LANES)
    pad_row = lambda v: _pad_cols(row(v), LANES)
    hid = jnp.arange(LANES)[:, None]
    e64 = jnp.tile((hid == (jnp.arange(inner)[None, :] // SSD_HEAD_DIM)).astype(BF16), (3, 1))
    e128 = jnp.tile((hid == (jnp.arange(heads * LANES)[None, :] // LANES)).astype(BF16), (3, 1))
    ltri = causal.astype(BF16)
    consts = (w_dt.astype(BF16), w_dt.T.astype(BF16), pad_row(ssd_dt_bias), pad_row(ssd_dt_bias).T,
              pad_row(ssd_a_log), pad_row(ssd_a_log).T, ssd_conv_w, row(ssd_conv_b),
              row(jnp.repeat(ssd_d, SSD_HEAD_DIM)), row(ssd_norm_w), e64, e128, ltri, ltri.T)
    y_norm = _ssd(x2d, h, bsz, consts, o1 // inner, o2 // conv_dim, inner, conv_dim)

    x1 = _merge(alpha, ya_g, y_norm, h, row(b_gate[d:]), ssd_w_out.astype(BF16), w_o.astype(BF16),
                x2d, row(ln1_g), row(ln1_b), gate_b_block)

    base, xbt, nsel, cw, r2, e2 = _peer_prep(
        alpha, x1, p2d, peer_w_q.T.astype(BF16), peer_k1.astype(BF16), peer_k2.astype(BF16),
        ple_w_gate.astype(BF16), row(ple_b_gate), ple_w_proj.astype(BF16))
    return _peer_experts(xbt, base, nsel, cw, r2, e2, _pack_bf16(peer_u),
                         _pack_bf16(peer_v, transpose=True), row(ln2_g), row(ln2_b))


def kernel(x, p, w_in, b_gate, gm_ln_g, gm_ln_b, gm_w_s, gm_b_s, gm_w_out, ssd_conv_w, ssd_conv_b,
           ssd_dt_bias, ssd_a_log, ssd_d, ssd_norm_w, ssd_w_out, w_o, ln1_g, ln1_b, peer_w_q,
           peer_k1, peer_k2, peer_u, peer_v, ple_w_proj, ple_w_gate, ple_b_gate, ln2_g, ln2_b):
    bsz, s, d = x.shape
    depth = w_in.shape[0]
    alpha = (2 * depth) ** 0.25
    x2d = x.reshape(bsz * s, d)
    for i in range(depth):
        x2d = _layer(x2d, p[i].reshape(bsz * s, -1), bsz, alpha, w_in[i], b_gate[i], gm_ln_g[i],
                     gm_ln_b[i], gm_w_s[i], gm_b_s[i], gm_w_out[i], ssd_conv_w[i], ssd_conv_b[i],
                     ssd_dt_bias[i], ssd_a_log[i], ssd_d[i], ssd_norm_w[i], ssd_w_out[i], w_o[i],
                     ln1_g[i], ln1_b[i], peer_w_q[i], peer_k1[i], peer_k2[i], peer_u[i], peer_v[i],
                     ple_w_proj[i], ple_w_gate[i], ple_b_gate[i], ln2_g[i], ln2_b[i])
    return x2d.reshape(bsz, s, d)
```
